```python
import jax, jax.numpy as jnp
from jax import lax
import numpy as np

D_MODEL = 1024
BATCH = 8
SEQ = 8192
DEPTH = 1
DEC_BATCH = 4
DEC_SEQ = 8192
PAST_LEN = 128

N_META = 16
GRID_W = 64
HEAD_DIM = 64
N_Q_HEADS = 8
N_KV_HEADS = 2
ATTN_WIDTH = N_Q_HEADS * HEAD_DIM
KV_WIDTH = N_KV_HEADS * HEAD_DIM
LRU_WIDTH = D_MODEL - ATTN_WIDTH
LRU_BLOCKS = 8
LRU_BLOCK_W = LRU_WIDTH // LRU_BLOCKS
IN_WIDTH = ATTN_WIDTH + 2 * KV_WIDTH + 2 * LRU_WIDTH
CONV_W = 4
CONV_LEFT = 2
LRU_C = 8.0
ROPE_AXIS_DIM = HEAD_DIM // 2
ROPE_THETA = 10000.0
Q_BLOCK = 128
FFN_HIDDEN = -(-8 * D_MODEL // (3 * 256)) * 256
EPS = 1e-6

kernel_name = "hymba_griffin_axial_gqa_encoder"


def rms_norm(x, g):
    xf = x.astype(jnp.float32)
    y = xf * lax.rsqrt(jnp.mean(xf * xf, axis=-1, keepdims=True) + EPS)
    return (y * g.astype(jnp.float32)).astype(x.dtype)


def axial_rope_tables(n_tokens):
    n_rows = n_tokens // GRID_W
    row = jnp.repeat(jnp.arange(n_rows), GRID_W).astype(jnp.float32)
    col = jnp.tile(jnp.arange(GRID_W), n_rows).astype(jnp.float32)
    freqs = ROPE_THETA ** (-jnp.arange(0, ROPE_AXIS_DIM, 2, dtype=jnp.float32) / ROPE_AXIS_DIM)
    ang = jnp.concatenate([row[:, None] * freqs, col[:, None] * freqs], axis=-1)
    ang = jnp.concatenate([jnp.zeros((N_META, ang.shape[1]), jnp.float32), ang], axis=0)
    return jnp.cos(ang), jnp.sin(ang)


def rotate_half_split(xp, c, s):
    h = xp.shape[-1] // 2
    x1, x2 = xp[..., :h], xp[..., h:]
    c = c[None, :, None, :]
    s = s[None, :, None, :]
    return jnp.concatenate([x1 * c - x2 * s, x2 * c + x1 * s], axis=-1)


def apply_axial_rope(x, cos, sin):
    xf = x.astype(jnp.float32)
    hf = ROPE_AXIS_DIM // 2
    xr = rotate_half_split(xf[..., :ROPE_AXIS_DIM], cos[:, :hf], sin[:, :hf])
    xc = rotate_half_split(xf[..., ROPE_AXIS_DIM:], cos[:, hf:], sin[:, hf:])
    return jnp.concatenate([xr, xc], axis=-1).astype(x.dtype)


def gqa_attention(q, k, v):
    B, L = q.shape[0], q.shape[1]
    G = N_Q_HEADS // N_KV_HEADS
    q = (q * (HEAD_DIM ** -0.5)).reshape(B, L, N_KV_HEADS, G, HEAD_DIM)

    def block(qb):
        s = jnp.einsum('bqkgd,bskd->bkgqs', qb, k).astype(jnp.float32)
        p = jax.nn.softmax(s, axis=-1).astype(v.dtype)
        return jnp.einsum('bkgqs,bskd->bqkgd', p, v)

    o_meta = block(q[:, :N_META]).reshape(B, N_META, ATTN_WIDTH)
    n_real = L - N_META
    nb = n_real // Q_BLOCK
    q_real = jnp.moveaxis(q[:, N_META:].reshape(B, nb, Q_BLOCK, N_KV_HEADS, G, HEAD_DIM), 1, 0)
    o_real = lax.map(block, q_real)
    o_real = jnp.moveaxis(o_real, 0, 1).reshape(B, n_real, ATTN_WIDTH)
    return jnp.concatenate([o_meta, o_real], axis=1)


def centred_depthwise_conv(x, w, b):
    L = x.shape[1]
    xp = jnp.pad(x, ((0, 0), (CONV_LEFT, CONV_W - 1 - CONV_LEFT), (0, 0)))
    out = b
    for j in range(CONV_W):
        out = out + xp[:, j:j + L] * w[j]
    return out


def _lin_combine(e1, e2):
    a1, b1 = e1
    a2, b2 = e2
    return a1 * a2, a2 * b1 + b2


def rg_lru_bidirectional(x, w_a, b_a, w_x, b_x, lam):
    B, L, W = x.shape
    xf = x.astype(jnp.float32)
    xb = xf.reshape(B, L, LRU_BLOCKS, LRU_BLOCK_W)
    ga = jnp.einsum('blhi,dhij->dblhj', xb, w_a.astype(jnp.float32)).reshape(2, B, L, W)
    gx = jnp.einsum('blhi,dhij->dblhj', xb, w_x.astype(jnp.float32)).reshape(2, B, L, W)
    r = jax.nn.sigmoid(ga + b_a.astype(jnp.float32)[:, None, None, :])
    i = jax.nn.sigmoid(gx + b_x.astype(jnp.float32)[:, None, None, :])
    log_a = -LRU_C * jax.nn.softplus(-lam.astype(jnp.float32))[:, None, None, :] * r
    a = jnp.exp(log_a)
    u = jnp.sqrt(-jnp.expm1(2.0 * log_a)) * (i * xf[None])
    h_f = lax.associative_scan(_lin_combine, (a[0], u[0]), axis=1)[1]
    h_b = lax.associative_scan(_lin_combine, (a[1], u[1]), axis=1, reverse=True)[1]
    return (h_f + h_b).astype(x.dtype)


def trunk(x, meta_tokens, norm_mix_g, w_in, q_norm_g, k_norm_g, conv_w, conv_b,
          lru_w_a, lru_b_a, lru_w_x, lru_b_x, lru_lam, attn_out_g, lru_out_g, w_out,
          norm_ffn_g, w_gate_up, w_down, final_norm_g):
    B, n_tok = x.shape[0], x.shape[1]
    cos, sin = axial_rope_tables(n_tok)
    h = jnp.concatenate([jnp.broadcast_to(meta_tokens.astype(x.dtype)[None], (B, N_META, D_MODEL)), x], axis=1)
    L = h.shape[1]
    splits = np.cumsum([ATTN_WIDTH, KV_WIDTH, KV_WIDTH, LRU_WIDTH]).tolist()
    for l in range(DEPTH):
        xn = rms_norm(h, norm_mix_g[l])
        proj = xn @ w_in[l]
        q, k, v, lru_in, lru_gate = jnp.split(proj, splits, axis=-1)
        q = rms_norm(q.reshape(B, L, N_Q_HEADS, HEAD_DIM), q_norm_g[l])
        k = rms_norm(k.reshape(B, L, N_KV_HEADS, HEAD_DIM), k_norm_g[l])
        v = v.reshape(B, L, N_KV_HEADS, HEAD_DIM)
        q = apply_axial_rope(q, cos, sin)
        k = apply_axial_rope(k, cos, sin)
        attn_o = rms_norm(gqa_attention(q, k, v), attn_out_g[l])
        c = centred_depthwise_conv(lru_in, conv_w[l], conv_b[l])
        rec = rg_lru_bidirectional(c, lru_w_a[l], lru_b_a[l], lru_w_x[l], lru_b_x[l], lru_lam[l])
        lru_o = rms_norm(rec * jax.nn.gelu(lru_gate), lru_out_g[l])
        h = h + jnp.concatenate([attn_o, lru_o], axis=-1) @ w_out[l]
        xn = rms_norm(h, norm_ffn_g[l])
        g, u = jnp.split(xn @ w_gate_up[l], 2, axis=-1)
        h = h + (jax.nn.silu(g) * u) @ w_down[l]
    h = rms_norm(h, final_norm_g)
    return h[:, N_META:]


def setup_inputs(seed: int = 0) -> dict:
    key = jax.random.key(seed)
    ks = jax.random.split(key, 24)
    f32 = jnp.float32
    nrm = lambda k, shape, scale: scale * jax.random.normal(k, shape, f32)
    gain = lambda k, shape: 1.0 + 0.02 * jax.random.normal(k, shape, f32)
    u = jax.random.uniform(ks[14], (DEPTH, 2, LRU_WIDTH), f32, minval=0.9, maxval=0.999)
    s = u ** (1.0 / LRU_C)
    lru_lam = jnp.log(s) - jnp.log1p(-s)
    return {
        "x_prompt": jax.random.normal(ks[0], (BATCH, SEQ, D_MODEL), f32),
        "x_sample": jax.random.normal(ks[1], (DEC_BATCH, DEC_SEQ, D_MODEL), f32),
        "meta_tokens": nrm(ks[2], (N_META, D_MODEL), 1.0),
        "norm_mix_g": gain(ks[3], (DEPTH, D_MODEL)),
        "w_in": nrm(ks[4], (DEPTH, D_MODEL, IN_WIDTH), D_MODEL ** -0.5),
        "q_norm_g": gain(ks[5], (DEPTH, HEAD_DIM)),
        "k_norm_g": gain(ks[6], (DEPTH, HEAD_DIM)),
        "conv_w": nrm(ks[7], (DEPTH, CONV_W, LRU_WIDTH), CONV_W ** -0.5),
        "conv_b": nrm(ks[8], (DEPTH, LRU_WIDTH), 0.01),
        "lru_w_a": nrm(ks[9], (DEPTH, 2, LRU_BLOCKS, LRU_BLOCK_W, LRU_BLOCK_W), LRU_BLOCK_W ** -0.5),
        "lru_b_a": nrm(ks[10], (DEPTH, 2, LRU_WIDTH), 0.01),
        "lru_w_x": nrm(ks[11], (DEPTH, 2, LRU_BLOCKS, LRU_BLOCK_W, LRU_BLOCK_W), LRU_BLOCK_W ** -0.5),
        "lru_b_x": nrm(ks[12], (DEPTH, 2, LRU_WIDTH), 0.01),
        "lru_lam": lru_lam,
        "attn_out_g": gain(ks[15], (DEPTH, ATTN_WIDTH)),
        "lru_out_g": gain(ks[16], (DEPTH, LRU_WIDTH)),
        "w_out": nrm(ks[17], (DEPTH, D_MODEL, D_MODEL), D_MODEL ** -0.5),
        "norm_ffn_g": gain(ks[18], (DEPTH, D_MODEL)),
        "w_gate_up": nrm(ks[19], (DEPTH, D_MODEL, 2 * FFN_HIDDEN), D_MODEL ** -0.5),
        "w_down": nrm(ks[20], (DEPTH, FFN_HIDDEN, D_MODEL), FFN_HIDDEN ** -0.5),
        "final_norm_g": gain(ks[21], (D_MODEL,)),
    }


def reference(x_prompt, x_sample, meta_tokens, norm_mix_g, w_in, q_norm_g, k_norm_g, conv_w, conv_b,
              lru_w_a, lru_b_a, lru_w_x, lru_b_x, lru_lam, attn_out_g, lru_out_g, w_out,
              norm_ffn_g, w_gate_up, w_down, final_norm_g):
    weights = (meta_tokens, norm_mix_g, w_in, q_norm_g, k_norm_g, conv_w, conv_b,
               lru_w_a, lru_b_a, lru_w_x, lru_b_x, lru_lam, attn_out_g, lru_out_g, w_out,
               norm_ffn_g, w_gate_up, w_down, final_norm_g)
    y_prompt = trunk(x_prompt, *weights)
    y_sample = trunk(x_sample, *weights)
    return (y_prompt, y_sample)
```

```python
import functools

import jax
import jax.numpy as jnp
import numpy as np
from jax import lax
from jax.experimental import pallas as pl
from jax.experimental.pallas import tpu as pltpu

F32 = jnp.float32
BF16 = jnp.bfloat16

N_META = 16
GRID_W = 64
HEAD_DIM = 64
N_Q_HEADS = 8
N_KV_HEADS = 2
Q_PER_KV = N_Q_HEADS // N_KV_HEADS
ATTN_WIDTH = N_Q_HEADS * HEAD_DIM
KV_WIDTH = N_KV_HEADS * HEAD_DIM
LRU_BLOCKS = 8
CONV_W = 4
CONV_LEFT = 2
LRU_C = 8.0
ROPE_AXIS_DIM = HEAD_DIM // 2
ROPE_THETA = 10000.0
EPS = 1e-6

LANES = 128
SUBLANES = 8
BF16_ROWS = 16
VMEM_LIMIT_BYTES = 56 * 1024 * 1024

LRU_LANES = LANES
SCAN_SEGMENTS = SUBLANES
CONV_HALO = SUBLANES
KV_CHUNK = 256
META_PAD = KV_CHUNK


def _rms_rows(x, g):
    ms = jnp.mean(x * x, axis=-1, keepdims=True)
    return x * lax.rsqrt(ms + EPS) * g


def _head_norm_rope(xT, g_col, cos, sin_signed):
    nh = xT.shape[0] // HEAD_DIM
    x = xT.reshape(nh, HEAD_DIM, xT.shape[1])
    ms = jnp.mean(x * x, axis=1, keepdims=True)
    x = x * lax.rsqrt(ms + EPS) * g_col[None]
    q = ROPE_AXIS_DIM // 2
    partner = jnp.concatenate(
        [x[:, q:2 * q], x[:, 0:q], x[:, 3 * q:4 * q], x[:, 2 * q:3 * q]], axis=1)
    x = x * cos[None] + partner * sin_signed[None]
    return x.reshape(xT.shape)


def _inproj_kernel(x_ref, g_ref, wqkv_ref, wlru_ref, qg_ref, kg_ref, cos_ref, sin_ref,
                   qT_ref, k_ref, vT_ref, lin_ref, lgate_ref):
    x = x_ref[0]
    xn = _rms_rows(x, g_ref[...]).astype(BF16)
    qkvT = lax.dot_general(wqkv_ref[...], xn, (((1,), (1,)), ((), ())),
                           preferred_element_type=F32)
    lru = jnp.dot(xn, wlru_ref[...], preferred_element_type=F32)
    cos = cos_ref[...]
    sin = sin_ref[...]
    qT = _head_norm_rope(qkvT[:ATTN_WIDTH], qg_ref[...], cos, sin) * (HEAD_DIM ** -0.5)
    qT_ref[0] = qT.astype(BF16)
    kT = _head_norm_rope(qkvT[ATTN_WIDTH:ATTN_WIDTH + KV_WIDTH], kg_ref[...], cos, sin)
    k_ref[0] = kT.T.astype(BF16)
    vT = qkvT[ATTN_WIDTH + KV_WIDTH:].astype(BF16)
    for j in range(vT_ref.shape[1]):
        vT_ref[0, j] = vT[:, j * KV_CHUNK:(j + 1) * KV_CHUNK]
    w = lin_ref.shape[2]
    lin_ref[0] = lru[:, :w]
    lgate_ref[0] = lru[:, w:]


def _inproj(x, g, wqkvT, wlru, qg_col, kg_col, cosT, sinT, tm):
    B, S, D = x.shape
    lw = wlru.shape[1] // 2
    nt = S // tm
    const = lambda b, i: (0, 0)
    return pl.pallas_call(
        _inproj_kernel,
        grid=(B, nt),
        in_specs=[
            pl.BlockSpec((1, tm, D), lambda b, i: (b, i, 0)),
            pl.BlockSpec((1, D), const),
            pl.BlockSpec(wqkvT.shape, const),
            pl.BlockSpec(wlru.shape, const),
            pl.BlockSpec((HEAD_DIM, 1), const),
            pl.BlockSpec((HEAD_DIM, 1), const),
            pl.BlockSpec((HEAD_DIM, tm), lambda b, i: (0, i)),
            pl.BlockSpec((HEAD_DIM, tm), lambda b, i: (0, i)),
        ],
        out_specs=[
            pl.BlockSpec((1, ATTN_WIDTH, tm), lambda b, i: (b, 0, i)),
            pl.BlockSpec((1, tm, KV_WIDTH), lambda b, i: (b, i, 0)),
            pl.BlockSpec((1, tm // KV_CHUNK, KV_WIDTH, KV_CHUNK), lambda b, i: (b, i, 0, 0)),
            pl.BlockSpec((1, tm, lw), lambda b, i: (b, i, 0)),
            pl.BlockSpec((1, tm, lw), lambda b, i: (b, i, 0)),
        ],
        out_shape=[
            jax.ShapeDtypeStruct((B, ATTN_WIDTH, S), BF16),
            jax.ShapeDtypeStruct((B, S, KV_WIDTH), BF16),
            jax.ShapeDtypeStruct((B, S // KV_CHUNK, KV_WIDTH, KV_CHUNK), BF16),
            jax.ShapeDtypeStruct((B, S, lw), F32),
            jax.ShapeDtypeStruct((B, S, lw), F32),
        ],
        compiler_params=pltpu.CompilerParams(
            dimension_semantics=("parallel", "parallel"), vmem_limit_bytes=VMEM_LIMIT_BYTES),
        name="inproj",
    )(x, g, wqkvT, wlru, qg_col, kg_col, cosT, sinT)


def _gelu_tanh(x):
    return 0.5 * x * (1.0 + jnp.tanh(np.sqrt(2.0 / np.pi).astype(np.float32) * (x + 0.044715 * (x * x * x))))


def _lru_kernel(xm_ref, x_ref, gate_ref, cw_ref, cb_ref, wg_ref, bg_ref, lam_ref, y_ref,
                X, Af, Uf, Ab, Ub, *, L, T, TY):
    S = L - N_META
    H = CONV_HALO
    seg = L // SCAN_SEGMENTS
    zeros = jnp.zeros((H, LRU_LANES), F32)
    X[0:H] = zeros
    X[H + L:H + L + H] = zeros
    X[H:H + N_META] = xm_ref[...]
    X[H + N_META:H + L] = x_ref[0]

    lam = lam_ref[...]
    z = -lam
    softplus = jnp.maximum(z, 0.0) + jnp.log1p(jnp.exp(-jnp.abs(z)))
    coef = -LRU_C * softplus
    cw = cw_ref[...]
    cb = cb_ref[...]
    bg = bg_ref[...]

    def gate_chunk(i, carry):
        r0 = pl.multiple_of(i * T, SUBLANES)
        xw = X[pl.ds(r0, T + 2 * H), :]
        c = cb
        for j in range(CONV_W):
            o = H - CONV_LEFT + j
            c = c + xw[o:o + T] * cw[j:j + 1]
        g = jnp.dot(c.astype(BF16), wg_ref[...], preferred_element_type=F32) + bg
        for d, (A, U) in enumerate(((Af, Uf), (Ab, Ub))):
            base = 2 * d * LRU_LANES
            r = jax.nn.sigmoid(g[:, base:base + LRU_LANES])
            ig = jax.nn.sigmoid(g[:, base + LRU_LANES:base + 2 * LRU_LANES])
            la = coef[d:d + 1] * r
            a = jnp.exp(la)
            mult = jnp.sqrt(-jnp.tanh(la) * (1.0 + a * a))
            A[pl.ds(r0, T), :] = a
            U[pl.ds(r0, T), :] = mult * (ig * c)
        return carry

    lax.fori_loop(0, L // T, gate_chunk, 0)

    def seg_rows(j):
        return pl.ds(j, SCAN_SEGMENTS, stride=seg)

    def scan_step(j, carry):
        hf, pf, hb, pb = carry
        a = Af[seg_rows(j), :]
        hf = a * hf + Uf[seg_rows(j), :]
        pf = a * pf
        Uf[seg_rows(j), :] = hf
        Af[seg_rows(j), :] = pf
        jb = seg - 1 - j
        a = Ab[seg_rows(jb), :]
        hb = a * hb + Ub[seg_rows(jb), :]
        pb = a * pb
        Ub[seg_rows(jb), :] = hb
        Ab[seg_rows(jb), :] = pb
        return hf, pf, hb, pb

    zero = jnp.zeros((SCAN_SEGMENTS, LRU_LANES), F32)
    one = jnp.ones((SCAN_SEGMENTS, LRU_LANES), F32)
    ef, pf, eb, pb = lax.fori_loop(0, seg, scan_step, (zero, one, zero, one), unroll=2)

    row = jnp.zeros((1, LRU_LANES), F32)
    rows = []
    for r in range(SCAN_SEGMENTS):
        rows.append(row)
        row = ef[r:r + 1] + pf[r:r + 1] * row
    cin_f = jnp.concatenate(rows, axis=0)
    row = jnp.zeros((1, LRU_LANES), F32)
    rows = []
    for r in reversed(range(SCAN_SEGMENTS)):
        rows.append(row)
        row = eb[r:r + 1] + pb[r:r + 1] * row
    cin_b = jnp.concatenate(rows[::-1], axis=0)

    def fix_step(j, carry):
        Uf[seg_rows(j), :] = (Uf[seg_rows(j), :] + Af[seg_rows(j), :] * cin_f
                              + Ub[seg_rows(j), :] + Ab[seg_rows(j), :] * cin_b)
        return carry

    lax.fori_loop(0, seg, fix_step, 0, unroll=2)

    def out_chunk(i, carry):
        r0 = pl.multiple_of(i * TY, SUBLANES)
        h = Uf[pl.ds(N_META + r0, TY), :]
        y_ref[0, pl.ds(r0, TY), :] = h * _gelu_tanh(gate_ref[0, pl.ds(r0, TY), :])
        return carry

    lax.fori_loop(0, S // TY, out_chunk, 0)


def _pick_chunk(n, align, cap):
    best = align
    for t in range(align, cap + 1, align):
        if n % t == 0:
            best = t
    return best


def _lru(xin_meta, xin, gate, cw, cb, wg, bg, lam):
    B, S, W = xin.shape
    L = S + N_META
    assert L % SCAN_SEGMENTS == 0 and W % LRU_LANES == 0
    ncg = W // LRU_LANES
    T = _pick_chunk(L, BF16_ROWS, 512)
    TY = _pick_chunk(S, SUBLANES, 512)
    assert L % T == 0 and S % TY == 0
    scratch_rows = L + 2 * CONV_HALO
    kern = functools.partial(_lru_kernel, L=L, T=T, TY=TY)
    return pl.pallas_call(
        kern,
        grid=(B, ncg),
        in_specs=[
            pl.BlockSpec((N_META, LRU_LANES), lambda b, c: (0, c)),
            pl.BlockSpec((1, S, LRU_LANES), lambda b, c: (b, 0, c)),
            pl.BlockSpec((1, S, LRU_LANES), lambda b, c: (b, 0, c)),
            pl.BlockSpec((CONV_W, LRU_LANES), lambda b, c: (0, c)),
            pl.BlockSpec((1, LRU_LANES), lambda b, c: (0, c)),
            pl.BlockSpec((None, LRU_LANES, 4 * LRU_LANES), lambda b, c: (c, 0, 0)),
            pl.BlockSpec((None, 1, 4 * LRU_LANES), lambda b, c: (c, 0, 0)),
            pl.BlockSpec((2, LRU_LANES), lambda b, c: (0, c)),
        ],
        out_specs=pl.BlockSpec((1, S, LRU_LANES), lambda b, c: (b, 0, c)),
        out_shape=jax.ShapeDtypeStruct((B, S, W), F32),
        scratch_shapes=[pltpu.VMEM((scratch_rows, LRU_LANES), F32)]
        + [pltpu.VMEM((L, LRU_LANES), F32) for _ in range(4)],
        compiler_params=pltpu.CompilerParams(
            dimension_semantics=("parallel", "parallel"), vmem_limit_bytes=VMEM_LIMIT_BYTES),
        name="lru",
    )(xin_meta, xin, gate, cw, cb, wg, bg, lam)


def _attn_kernel(qT_ref, k_ref, vT_ref, km_ref, vTm_ref, og_ref, o_ref, qext, m_s, l_s, acc_s):
    tq = qT_ref.shape[2]
    nchunks = vT_ref.shape[1]
    zeros = jnp.zeros((HEAD_DIM, tq), BF16)
    for h in range(N_Q_HEADS):
        qh = qT_ref[0, h * HEAD_DIM:(h + 1) * HEAD_DIM, :]
        parts = [zeros] * N_KV_HEADS
        parts[h // Q_PER_KV] = qh
        qext[h] = jnp.concatenate(parts, axis=0)

    km = km_ref[...]
    vTm = vTm_ref[...]
    for h in range(N_Q_HEADS):
        g = h // Q_PER_KV
        s = jnp.dot(km, qext[h], preferred_element_type=F32)
        m = jnp.max(s, axis=0, keepdims=True)
        p = jnp.exp(s - m)
        m_s[h] = m
        l_s[h] = jnp.sum(p, axis=0, keepdims=True)
        acc_s[h] = jnp.dot(vTm[g * HEAD_DIM:(g + 1) * HEAD_DIM], p.astype(BF16),
                           preferred_element_type=F32)

    def kv_step(c, carry):
        kc = k_ref[0, pl.ds(pl.multiple_of(c * KV_CHUNK, KV_CHUNK), KV_CHUNK), :]
        vc = vT_ref[0, c]
        for h in range(N_Q_HEADS):
            g = h // Q_PER_KV
            s = jnp.dot(kc, qext[h], preferred_element_type=F32)
            m_old = m_s[h]
            m_new = jnp.maximum(m_old, jnp.max(s, axis=0, keepdims=True))
            alpha = jnp.exp(m_old - m_new)
            p = jnp.exp(s - m_new)
            l_s[h] = alpha * l_s[h] + jnp.sum(p, axis=0, keepdims=True)
            pv = jnp.dot(vc[g * HEAD_DIM:(g + 1) * HEAD_DIM], p.astype(BF16),
                         preferred_element_type=F32)
            acc_s[h] = alpha * acc_s[h] + pv
            m_s[h] = m_new
        return carry

    lax.fori_loop(0, nchunks, kv_step, 0)

    oT = jnp.concatenate([acc_s[h] / l_s[h] for h in range(N_Q_HEADS)], axis=0)
    ms = jnp.mean(oT * oT, axis=0, keepdims=True)
    oT = oT * lax.rsqrt(ms + EPS) * og_ref[...]
    o_ref[0] = oT.T.astype(BF16)


def _attention(qT, k, vT, k_meta, vT_meta, og_col, tq):
    B, _, S = qT.shape
    nchunks = S // KV_CHUNK
    return pl.pallas_call(
        _attn_kernel,
        grid=(B, S // tq),
        in_specs=[
            pl.BlockSpec((1, ATTN_WIDTH, tq), lambda b, i: (b, 0, i)),
            pl.BlockSpec((1, S, KV_WIDTH), lambda b, i: (b, 0, 0)),
            pl.BlockSpec((1, nchunks, KV_WIDTH, KV_CHUNK), lambda b, i: (b, 0, 0, 0)),
            pl.BlockSpec((N_META, KV_WIDTH), lambda b, i: (0, 0)),
            pl.BlockSpec((KV_WIDTH, N_META), lambda b, i: (0, 0)),
            pl.BlockSpec((ATTN_WIDTH, 1), lambda b, i: (0, 0)),
        ],
        out_specs=pl.BlockSpec((1, tq, ATTN_WIDTH), lambda b, i: (b, i, 0)),
        out_shape=jax.ShapeDtypeStruct((B, S, ATTN_WIDTH), BF16),
        scratch_shapes=[
            pltpu.VMEM((N_Q_HEADS, KV_WIDTH, tq), BF16),
            pltpu.VMEM((N_Q_HEADS, 1, tq), F32),
            pltpu.VMEM((N_Q_HEADS, 1, tq), F32),
            pltpu.VMEM((N_Q_HEADS, HEAD_DIM, tq), F32),
        ],
        compiler_params=pltpu.CompilerParams(
            dimension_semantics=("parallel", "parallel"), vmem_limit_bytes=VMEM_LIMIT_BYTES),
        name="attention",
    )(qT, k, vT, k_meta, vT_meta, og_col)


def _post_kernel(x_ref, ao_ref, ly_ref, lg_ref, woa_ref, wol_ref, fg_ref, wg_ref, wu_ref, wd_ref,
                 og_ref, o_ref, *, n_ffn_chunks):
    lo = _rms_rows(ly_ref[0], lg_ref[...]).astype(BF16)
    h = (x_ref[0]
         + jnp.dot(ao_ref[0], woa_ref[...], preferred_element_type=F32)
         + jnp.dot(lo, wol_ref[...], preferred_element_type=F32))
    xn = _rms_rows(h, fg_ref[...]).astype(BF16)
    fw = wg_ref.shape[1] // n_ffn_chunks
    for c in range(n_ffn_chunks):
        gate = jnp.dot(xn, wg_ref[:, c * fw:(c + 1) * fw], preferred_element_type=F32)
        up = jnp.dot(xn, wu_ref[:, c * fw:(c + 1) * fw], preferred_element_type=F32)
        act = (gate * jax.nn.sigmoid(gate) * up).astype(BF16)
        h = h + jnp.dot(act, wd_ref[c * fw:(c + 1) * fw, :], preferred_element_type=F32)
    o_ref[0] = _rms_rows(h, og_ref[...])


def _post(x, attn_o, lru_y, lru_g, wo_a, wo_l, ffn_g, w_gate, w_up, w_down, out_g, tm):
    B, S, D = x.shape
    F = w_gate.shape[1]
    n_ffn_chunks = 1
    const = lambda b, i: (0, 0)
    resident = functools.partial(pl.BlockSpec, index_map=const, pipeline_mode=pl.Buffered(1))
    row = lambda w: pl.BlockSpec((1, tm, w), lambda b, i: (b, i, 0))
    return pl.pallas_call(
        functools.partial(_post_kernel, n_ffn_chunks=n_ffn_chunks),
        grid=(B, S // tm),
        in_specs=[
            row(D), row(ATTN_WIDTH), row(lru_y.shape[2]),
            resident((1, lru_y.shape[2])),
            resident(wo_a.shape), resident(wo_l.shape),
            resident((1, D)),
            resident(w_gate.shape), resident(w_up.shape), resident(w_down.shape),
            resident((1, D)),
        ],
        out_specs=row(D),
        out_shape=jax.ShapeDtypeStruct((B, S, D), F32),
        compiler_params=pltpu.CompilerParams(
            dimension_semantics=("parallel", "parallel"), vmem_limit_bytes=VMEM_LIMIT_BYTES),
        name="post",
    )(x, attn_o, lru_y, lru_g, wo_a, wo_l, ffn_g, w_gate, w_up, w_down, out_g)


def _rope_tables(n_tokens):
    n_rows = n_tokens // GRID_W
    row = jnp.repeat(jnp.arange(n_rows), GRID_W).astype(F32)
    col = jnp.tile(jnp.arange(GRID_W), n_rows).astype(F32)
    freqs = ROPE_THETA ** (-jnp.arange(0, ROPE_AXIS_DIM, 2, dtype=F32) / ROPE_AXIS_DIM)
    ar = freqs[:, None] * row[None, :]
    ac = freqs[:, None] * col[None, :]
    cos = jnp.concatenate([jnp.cos(ar), jnp.cos(ar), jnp.cos(ac), jnp.cos(ac)], axis=0)
    sin = jnp.concatenate([-jnp.sin(ar), jnp.sin(ar), -jnp.sin(ac), jnp.sin(ac)], axis=0)
    return cos, sin


def _block_diag_pairs(w):
    nb, bw, _ = w.shape
    w = w.reshape(nb // 2, 2, bw, bw)
    z = jnp.zeros((nb // 2, bw, bw), w.dtype)
    top = jnp.concatenate([w[:, 0], z], axis=2)
    bot = jnp.concatenate([z, w[:, 1]], axis=2)
    return jnp.concatenate([top, bot], axis=1)


def _row_tile(S):
    return 512 if S % 512 == 0 else S


def _trunk(x, prep):
    B, S, D = x.shape
    tm = _row_tile(S)
    cosT, sinT = _rope_tables(S)
    qT, k, vT, lin, lgate = _inproj(x, prep["mix_g"], prep["wqkvT"], prep["wlru"], prep["qg"], prep["kg"],
                                    cosT, sinT, tm)
    lru_y = _lru(prep["lin_meta"], lin, lgate, prep["conv_w"], prep["conv_b"], prep["w_gates"],
                 prep["b_gates"], prep["lam"])
    attn_o = _attention(qT, k, vT, prep["k_meta"], prep["vT_meta"], prep["attn_g"], tm)
    return _post(x, attn_o, lru_y, prep["lru_g"], prep["wo_a"], prep["wo_l"], prep["ffn_g"],
                 prep["w_gate"], prep["w_up"], prep["w_down"], prep["final_g"], tm)


def kernel(x_prompt, x_sample, meta_tokens, norm_mix_g, w_in, q_norm_g, k_norm_g, conv_w, conv_b, lru_w_a, lru_b_a, lru_w_x, lru_b_x, lru_lam, attn_out_g, lru_out_g, w_out, norm_ffn_g, w_gate_up, w_down, final_norm_g):
    assert w_in.shape[0] == 1, "single-layer trunk"
    D = x_prompt.shape[-1]
    lw = D - ATTN_WIDTH
    qkv_w = ATTN_WIDTH + 2 * KV_WIDTH
    w = w_in[0]
    prep = {
        "mix_g": norm_mix_g[0][None],
        "wqkvT": w[:, :qkv_w].T.astype(BF16),
        "wlru": w[:, qkv_w:].astype(BF16),
        "qg": q_norm_g[0][:, None],
        "kg": k_norm_g[0][:, None],
        "conv_w": conv_w[0],
        "conv_b": conv_b[0][None],
        "lam": lru_lam[0],
        "attn_g": attn_out_g[0][:, None],
        "lru_g": lru_out_g[0][None],
        "wo_a": w_out[0][:ATTN_WIDTH].astype(BF16),
        "wo_l": w_out[0][ATTN_WIDTH:].astype(BF16),
        "ffn_g": norm_ffn_g[0][None],
        "w_gate": w_gate_up[0][:, :w_gate_up.shape[2] // 2].astype(BF16),
        "w_up": w_gate_up[0][:, w_gate_up.shape[2] // 2:].astype(BF16),
        "w_down": w_down[0].astype(BF16),
        "final_g": final_norm_g[None],
    }
    wa = [_block_diag_pairs(lru_w_a[0, d]) for d in range(2)]
    wx = [_block_diag_pairs(lru_w_x[0, d]) for d in range(2)]
    prep["w_gates"] = jnp.concatenate([wa[0], wx[0], wa[1], wx[1]], axis=2).astype(BF16)
    ncg = lw // LRU_LANES
    bias = [b.reshape(ncg, 1, LRU_LANES) for b in (lru_b_a[0, 0], lru_b_x[0, 0], lru_b_a[0, 1], lru_b_x[0, 1])]
    prep["b_gates"] = jnp.concatenate(bias, axis=2)

    meta = jnp.zeros((1, META_PAD, D), F32).at[0, :N_META].set(meta_tokens)
    ones = jnp.ones((HEAD_DIM, META_PAD), F32)
    _, k_m, vT_m, lin_m, _ = _inproj(meta, prep["mix_g"], prep["wqkvT"], prep["wlru"], prep["qg"], prep["kg"],
                                     ones, jnp.zeros_like(ones), META_PAD)
    prep["k_meta"] = k_m[0, :N_META]
    prep["vT_meta"] = vT_m[0, 0, :, :N_META]
    prep["lin_meta"] = lin_m[0, :N_META]
    return _trunk(x_prompt, prep), _trunk(x_sample, prep)
```

```python
import functools

import jax
import jax.numpy as jnp
import numpy as np
from jax import lax
from jax.experimental import pallas as pl
from jax.experimental.pallas import tpu as pltpu

F32 = jnp.float32
BF16 = jnp.bfloat16

N_META = 16
GRID_W = 64
HEAD_DIM = 64
N_Q_HEADS = 8
N_KV_HEADS = 2
Q_PER_KV = N_Q_HEADS // N_KV_HEADS
ATTN_WIDTH = N_Q_HEADS * HEAD_DIM
KV_WIDTH = N_KV_HEADS * HEAD_DIM
LRU_BLOCKS = 8
CONV_W = 4
CONV_LEFT = 2
LRU_C = 8.0
ROPE_AXIS_DIM = HEAD_DIM // 2
ROPE_THETA = 10000.0
EPS = 1e-6
LOG2_E = float(np.log2(np.e))

LANES = 128
SUBLANES = 8
BF16_ROWS = 16
VMEM_LIMIT_BYTES = 56 * 1024 * 1024

LRU_LANES = LANES
SCAN_SEGMENTS = SUBLANES
CONV_HALO = SUBLANES
KV_CHUNK = 256
META_PAD = KV_CHUNK


def _rms_rows(x, g):
    ms = jnp.mean(x * x, axis=-1, keepdims=True)
    return x * lax.rsqrt(ms + EPS) * g


def _head_norm_rope(xT, g_col, cos, sin_signed):
    nh = xT.shape[0] // HEAD_DIM
    x = xT.reshape(nh, HEAD_DIM, xT.shape[1])
    ms = jnp.mean(x * x, axis=1, keepdims=True)
    x = x * lax.rsqrt(ms + EPS) * g_col[None]
    q = ROPE_AXIS_DIM // 2
    partner = jnp.concatenate(
        [x[:, q:2 * q], x[:, 0:q], x[:, 3 * q:4 * q], x[:, 2 * q:3 * q]], axis=1)
    x = x * cos[None] + partner * sin_signed[None]
    return x.reshape(xT.shape)


def _inproj_kernel(x_ref, g_ref, wqkv_ref, wlru_ref, qg_ref, kg_ref, cos_ref, sin_ref,
                   qT_ref, k_ref, vT_ref, lin_ref, lgate_ref):
    x = x_ref[0]
    xn = _rms_rows(x, g_ref[...]).astype(BF16)
    qkvT = lax.dot_general(wqkv_ref[...], xn, (((1,), (1,)), ((), ())),
                           preferred_element_type=F32)
    lru = jnp.dot(xn, wlru_ref[...], preferred_element_type=F32)
    cos = cos_ref[...]
    sin = sin_ref[...]
    qT = _head_norm_rope(qkvT[:ATTN_WIDTH], qg_ref[...], cos, sin) * (HEAD_DIM ** -0.5 * LOG2_E)
    qT_ref[0] = qT.astype(BF16)
    kT = _head_norm_rope(qkvT[ATTN_WIDTH:ATTN_WIDTH + KV_WIDTH], kg_ref[...], cos, sin)
    k_ref[0] = kT.T.astype(BF16)
    vT = qkvT[ATTN_WIDTH + KV_WIDTH:].astype(BF16)
    for j in range(vT_ref.shape[1]):
        vT_ref[0, j] = vT[:, j * KV_CHUNK:(j + 1) * KV_CHUNK]
    w = lin_ref.shape[2]
    lin_ref[0] = lru[:, :w]
    lgate_ref[0] = lru[:, w:]


def _inproj(x, g, wqkvT, wlru, qg_col, kg_col, cosT, sinT, tm):
    B, S, D = x.shape
    lw = wlru.shape[1] // 2
    nt = S // tm
    const = lambda b, i: (0, 0)
    return pl.pallas_call(
        _inproj_kernel,
        grid=(B, nt),
        in_specs=[
            pl.BlockSpec((1, tm, D), lambda b, i: (b, i, 0)),
            pl.BlockSpec((1, D), const),
            pl.BlockSpec(wqkvT.shape, const),
            pl.BlockSpec(wlru.shape, const),
            pl.BlockSpec((HEAD_DIM, 1), const),
            pl.BlockSpec((HEAD_DIM, 1), const),
            pl.BlockSpec((HEAD_DIM, tm), lambda b, i: (0, i)),
            pl.BlockSpec((HEAD_DIM, tm), lambda b, i: (0, i)),
        ],
        out_specs=[
            pl.BlockSpec((1, ATTN_WIDTH, tm), lambda b, i: (b, 0, i)),
            pl.BlockSpec((1, tm, KV_WIDTH), lambda b, i: (b, i, 0)),
            pl.BlockSpec((1, tm // KV_CHUNK, KV_WIDTH, KV_CHUNK), lambda b, i: (b, i, 0, 0)),
            pl.BlockSpec((1, tm, lw), lambda b, i: (b, i, 0)),
            pl.BlockSpec((1, tm, lw), lambda b, i: (b, i, 0)),
        ],
        out_shape=[
            jax.ShapeDtypeStruct((B, ATTN_WIDTH, S), BF16),
            jax.ShapeDtypeStruct((B, S, KV_WIDTH), BF16),
            jax.ShapeDtypeStruct((B, S // KV_CHUNK, KV_WIDTH, KV_CHUNK), BF16),
            jax.ShapeDtypeStruct((B, S, lw), F32),
            jax.ShapeDtypeStruct((B, S, lw), F32),
        ],
        compiler_params=pltpu.CompilerParams(
            dimension_semantics=("parallel", "parallel"), vmem_limit_bytes=VMEM_LIMIT_BYTES),
        name="inproj",
    )(x, g, wqkvT, wlru, qg_col, kg_col, cosT, sinT)


def _gelu_tanh(x):
    return 0.5 * x * (1.0 + jnp.tanh(np.sqrt(2.0 / np.pi).astype(np.float32) * (x + 0.044715 * (x * x * x))))


def _lru_kernel(xm_ref, x_ref, gate_ref, cw_ref, cb_ref, wg_ref, bg_ref, lam_ref, y_ref,
                X, Af, Uf, Ab, Ub, *, L, T, TY):
    S = L - N_META
    H = CONV_HALO
    seg = L // SCAN_SEGMENTS
    zeros = jnp.zeros((H, LRU_LANES), F32)
    X[0:H] = zeros
    X[H + L:H + L + H] = zeros
    X[H:H + N_META] = xm_ref[...]
    X[H + N_META:H + L] = x_ref[0]

    lam = lam_ref[...]
    z = -lam
    softplus = jnp.maximum(z, 0.0) + jnp.log1p(jnp.exp(-jnp.abs(z)))
    coef = -LRU_C * softplus
    cw = cw_ref[...]
    cb = cb_ref[...]
    bg = bg_ref[...]

    def gate_chunk(i, carry):
        r0 = pl.multiple_of(i * T, SUBLANES)
        xw = X[pl.ds(r0, T + 2 * H), :]
        c = cb
        for j in range(CONV_W):
            o = H - CONV_LEFT + j
            c = c + xw[o:o + T] * cw[j:j + 1]
        g = jnp.dot(c.astype(BF16), wg_ref[...], preferred_element_type=F32) + bg
        for d, (A, U) in enumerate(((Af, Uf), (Ab, Ub))):
            base = 2 * d * LRU_LANES
            r = jax.nn.sigmoid(g[:, base:base + LRU_LANES])
            ig = jax.nn.sigmoid(g[:, base + LRU_LANES:base + 2 * LRU_LANES])
            la = coef[d:d + 1] * r
            a = jnp.exp(la)
            mult = jnp.sqrt(-jnp.tanh(la) * (1.0 + a * a))
            A[pl.ds(r0, T), :] = a
            U[pl.ds(r0, T), :] = mult * (ig * c)
        return carry

    lax.fori_loop(0, L // T, gate_chunk, 0)

    def seg_rows(j):
        return pl.ds(j, SCAN_SEGMENTS, stride=seg)

    def scan_step(j, carry):
        hf, pf, hb, pb = carry
        a = Af[seg_rows(j), :]
        hf = a * hf + Uf[seg_rows(j), :]
        pf = a * pf
        Uf[seg_rows(j), :] = hf
        Af[seg_rows(j), :] = pf
        jb = seg - 1 - j
        a = Ab[seg_rows(jb), :]
        hb = a * hb + Ub[seg_rows(jb), :]
        pb = a * pb
        Ub[seg_rows(jb), :] = hb
        Ab[seg_rows(jb), :] = pb
        return hf, pf, hb, pb

    zero = jnp.zeros((SCAN_SEGMENTS, LRU_LANES), F32)
    one = jnp.ones((SCAN_SEGMENTS, LRU_LANES), F32)
    ef, pf, eb, pb = lax.fori_loop(0, seg, scan_step, (zero, one, zero, one), unroll=2)

    row = jnp.zeros((1, LRU_LANES), F32)
    rows = []
    for r in range(SCAN_SEGMENTS):
        rows.append(row)
        row = ef[r:r + 1] + pf[r:r + 1] * row
    cin_f = jnp.concatenate(rows, axis=0)
    row = jnp.zeros((1, LRU_LANES), F32)
    rows = []
    for r in reversed(range(SCAN_SEGMENTS)):
        rows.append(row)
        row = eb[r:r + 1] + pb[r:r + 1] * row
    cin_b = jnp.concatenate(rows[::-1], axis=0)

    def fix_step(j, carry):
        Uf[seg_rows(j), :] = (Uf[seg_rows(j), :] + Af[seg_rows(j), :] * cin_f
                              + Ub[seg_rows(j), :] + Ab[seg_rows(j), :] * cin_b)
        return carry

    lax.fori_loop(0, seg, fix_step, 0, unroll=2)

    def out_chunk(i, carry):
        r0 = pl.multiple_of(i * TY, SUBLANES)
        h = Uf[pl.ds(N_META + r0, TY), :]
        y_ref[0, pl.ds(r0, TY), :] = h * _gelu_tanh(gate_ref[0, pl.ds(r0, TY), :])
        return carry

    lax.fori_loop(0, S // TY, out_chunk, 0)


def _pick_chunk(n, align, cap):
    best = align
    for t in range(align, cap + 1, align):
        if n % t == 0:
            best = t
    return best


def _lru(xin_meta, xin, gate, cw, cb, wg, bg, lam):
    B, S, W = xin.shape
    L = S + N_META
    assert L % SCAN_SEGMENTS == 0 and W % LRU_LANES == 0
    ncg = W // LRU_LANES
    T = _pick_chunk(L, BF16_ROWS, 512)
    TY = _pick_chunk(S, SUBLANES, 512)
    assert L % T == 0 and S % TY == 0
    scratch_rows = L + 2 * CONV_HALO
    kern = functools.partial(_lru_kernel, L=L, T=T, TY=TY)
    return pl.pallas_call(
        kern,
        grid=(B, ncg),
        in_specs=[
            pl.BlockSpec((N_META, LRU_LANES), lambda b, c: (0, c)),
            pl.BlockSpec((1, S, LRU_LANES), lambda b, c: (b, 0, c)),
            pl.BlockSpec((1, S, LRU_LANES), lambda b, c: (b, 0, c)),
            pl.BlockSpec((CONV_W, LRU_LANES), lambda b, c: (0, c)),
            pl.BlockSpec((1, LRU_LANES), lambda b, c: (0, c)),
            pl.BlockSpec((None, LRU_LANES, 4 * LRU_LANES), lambda b, c: (c, 0, 0)),
            pl.BlockSpec((None, 1, 4 * LRU_LANES), lambda b, c: (c, 0, 0)),
            pl.BlockSpec((2, LRU_LANES), lambda b, c: (0, c)),
        ],
        out_specs=pl.BlockSpec((1, S, LRU_LANES), lambda b, c: (b, 0, c)),
        out_shape=jax.ShapeDtypeStruct((B, S, W), F32),
        scratch_shapes=[pltpu.VMEM((scratch_rows, LRU_LANES), F32)]
        + [pltpu.VMEM((L, LRU_LANES), F32) for _ in range(4)],
        compiler_params=pltpu.CompilerParams(
            dimension_semantics=("parallel", "parallel"), vmem_limit_bytes=VMEM_LIMIT_BYTES),
        name="lru",
    )(xin_meta, xin, gate, cw, cb, wg, bg, lam)


def _attn_kernel(qT_ref, k_ref, vT_ref, km_ref, vTm_ref, og_ref, o_ref, qext, m_s, l_s, acc_s):
    tq = qT_ref.shape[2]
    nchunks = vT_ref.shape[1]
    zeros = jnp.zeros((HEAD_DIM, tq), BF16)
    for h in range(N_Q_HEADS):
        qh = qT_ref[0, h * HEAD_DIM:(h + 1) * HEAD_DIM, :]
        parts = [zeros] * N_KV_HEADS
        parts[h // Q_PER_KV] = qh
        qext[h] = jnp.concatenate(parts, axis=0)

    km = km_ref[...]
    vTm = vTm_ref[...]
    for h in range(N_Q_HEADS):
        g = h // Q_PER_KV
        s = jnp.dot(km, qext[h], preferred_element_type=F32)
        m = jnp.max(s, axis=0, keepdims=True)
        p = jnp.exp2(s - m)
        m_s[h] = m
        l_s[h] = jnp.sum(p, axis=0, keepdims=True)
        acc_s[h] = jnp.dot(vTm[g * HEAD_DIM:(g + 1) * HEAD_DIM], p.astype(BF16),
                           preferred_element_type=F32)

    def kv_step(c, carry):
        kc = k_ref[0, pl.ds(pl.multiple_of(c * KV_CHUNK, KV_CHUNK), KV_CHUNK), :]
        vc = vT_ref[0, c]
        s_next = jnp.dot(kc, qext[0], preferred_element_type=F32)
        for h in range(N_Q_HEADS):
            g = h // Q_PER_KV
            s = s_next
            if h + 1 < N_Q_HEADS:
                s_next = jnp.dot(kc, qext[h + 1], preferred_element_type=F32)
            m_old = m_s[h]
            m_new = jnp.maximum(m_old, jnp.max(s, axis=0, keepdims=True))
            alpha = jnp.exp2(m_old - m_new)
            p = jnp.exp2(s - m_new)
            l_s[h] = alpha * l_s[h] + jnp.sum(p, axis=0, keepdims=True)
            pv = jnp.dot(vc[g * HEAD_DIM:(g + 1) * HEAD_DIM], p.astype(BF16),
                         preferred_element_type=F32)
            acc_s[h] = alpha * acc_s[h] + pv
            m_s[h] = m_new
        return carry

    lax.fori_loop(0, nchunks, kv_step, 0)

    oT = jnp.concatenate([acc_s[h] / l_s[h] for h in range(N_Q_HEADS)], axis=0)
    ms = jnp.mean(oT * oT, axis=0, keepdims=True)
    oT = oT * lax.rsqrt(ms + EPS) * og_ref[...]
    o_ref[0] = oT.T.astype(BF16)


def _attention(qT, k, vT, k_meta, vT_meta, og_col, tq):
    B, _, S = qT.shape
    nchunks = S // KV_CHUNK
    return pl.pallas_call(
        _attn_kernel,
        grid=(B, S // tq),
        in_specs=[
            pl.BlockSpec((1, ATTN_WIDTH, tq), lambda b, i: (b, 0, i)),
            pl.BlockSpec((1, S, KV_WIDTH), lambda b, i: (b, 0, 0)),
            pl.BlockSpec((1, nchunks, KV_WIDTH, KV_CHUNK), lambda b, i: (b, 0, 0, 0)),
            pl.BlockSpec((N_META, KV_WIDTH), lambda b, i: (0, 0)),
            pl.BlockSpec((KV_WIDTH, N_META), lambda b, i: (0, 0)),
            pl.BlockSpec((ATTN_WIDTH, 1), lambda b, i: (0, 0)),
        ],
        out_specs=pl.BlockSpec((1, tq, ATTN_WIDTH), lambda b, i: (b, i, 0)),
        out_shape=jax.ShapeDtypeStruct((B, S, ATTN_WIDTH), BF16),
        scratch_shapes=[
            pltpu.VMEM((N_Q_HEADS, KV_WIDTH, tq), BF16),
            pltpu.VMEM((N_Q_HEADS, 1, tq), F32),
            pltpu.VMEM((N_Q_HEADS, 1, tq), F32),
            pltpu.VMEM((N_Q_HEADS, HEAD_DIM, tq), F32),
        ],
        compiler_params=pltpu.CompilerParams(
            dimension_semantics=("parallel", "parallel"), vmem_limit_bytes=VMEM_LIMIT_BYTES),
        name="attention",
    )(qT, k, vT, k_meta, vT_meta, og_col)


def _post_kernel(x_ref, ao_ref, ly_ref, lg_ref, woa_ref, wol_ref, fg_ref, wg_ref, wu_ref, wd_ref,
                 og_ref, o_ref, *, n_ffn_chunks):
    lo = _rms_rows(ly_ref[0], lg_ref[...]).astype(BF16)
    h = (x_ref[0]
         + jnp.dot(ao_ref[0], woa_ref[...], preferred_element_type=F32)
         + jnp.dot(lo, wol_ref[...], preferred_element_type=F32))
    xn = _rms_rows(h, fg_ref[...]).astype(BF16)
    fw = wg_ref.shape[1] // n_ffn_chunks
    for c in range(n_ffn_chunks):
        gate = jnp.dot(xn, wg_ref[:, c * fw:(c + 1) * fw], preferred_element_type=F32)
        up = jnp.dot(xn, wu_ref[:, c * fw:(c + 1) * fw], preferred_element_type=F32)
        act = (gate * jax.nn.sigmoid(gate) * up).astype(BF16)
        h = h + jnp.dot(act, wd_ref[c * fw:(c + 1) * fw, :], preferred_element_type=F32)
    o_ref[0] = _rms_rows(h, og_ref[...])


def _post(x, attn_o, lru_y, lru_g, wo_a, wo_l, ffn_g, w_gate, w_up, w_down, out_g, tm):
    B, S, D = x.shape
    F = w_gate.shape[1]
    n_ffn_chunks = 1
    const = lambda b, i: (0, 0)
    resident = functools.partial(pl.BlockSpec, index_map=const, pipeline_mode=pl.Buffered(1))
    row = lambda w: pl.BlockSpec((1, tm, w), lambda b, i: (b, i, 0))
    return pl.pallas_call(
        functools.partial(_post_kernel, n_ffn_chunks=n_ffn_chunks),
        grid=(B, S // tm),
        in_specs=[
            row(D), row(ATTN_WIDTH), row(lru_y.shape[2]),
            resident((1, lru_y.shape[2])),
            resident(wo_a.shape), resident(wo_l.shape),
            resident((1, D)),
            resident(w_gate.shape), resident(w_up.shape), resident(w_down.shape),
            resident((1, D)),
        ],
        out_specs=row(D),
        out_shape=jax.ShapeDtypeStruct((B, S, D), F32),
        compiler_params=pltpu.CompilerParams(
            dimension_semantics=("parallel", "parallel"), vmem_limit_bytes=VMEM_LIMIT_BYTES),
        name="post",
    )(x, attn_o, lru_y, lru_g, wo_a, wo_l, ffn_g, w_gate, w_up, w_down, out_g)


def _rope_tables(n_tokens):
    n_rows = n_tokens // GRID_W
    row = jnp.repeat(jnp.arange(n_rows), GRID_W).astype(F32)
    col = jnp.tile(jnp.arange(GRID_W), n_rows).astype(F32)
    freqs = ROPE_THETA ** (-jnp.arange(0, ROPE_AXIS_DIM, 2, dtype=F32) / ROPE_AXIS_DIM)
    ar = freqs[:, None] * row[None, :]
    ac = freqs[:, None] * col[None, :]
    cos = jnp.concatenate([jnp.cos(ar), jnp.cos(ar), jnp.cos(ac), jnp.cos(ac)], axis=0)
    sin = jnp.concatenate([-jnp.sin(ar), jnp.sin(ar), -jnp.sin(ac), jnp.sin(ac)], axis=0)
    return cos, sin


def _block_diag_pairs(w):
    nb, bw, _ = w.shape
    w = w.reshape(nb // 2, 2, bw, bw)
    z = jnp.zeros((nb // 2, bw, bw), w.dtype)
    top = jnp.concatenate([w[:, 0], z], axis=2)
    bot = jnp.concatenate([z, w[:, 1]], axis=2)
    return jnp.concatenate([top, bot], axis=1)


def _row_tile(S):
    return 512 if S % 512 == 0 else S


def _trunk(x, prep):
    B, S, D = x.shape
    tm = _row_tile(S)
    cosT, sinT = _rope_tables(S)
    qT, k, vT, lin, lgate = _inproj(x, prep["mix_g"], prep["wqkvT"], prep["wlru"], prep["qg"], prep["kg"],
                                    cosT, sinT, tm)
    lru_y = _lru(prep["lin_meta"], lin, lgate, prep["conv_w"], prep["conv_b"], prep["w_gates"],
                 prep["b_gates"], prep["lam"])
    attn_o = _attention(qT, k, vT, prep["k_meta"], prep["vT_meta"], prep["attn_g"], tm)
    return _post(x, attn_o, lru_y, prep["lru_g"], prep["wo_a"], prep["wo_l"], prep["ffn_g"],
                 prep["w_gate"], prep["w_up"], prep["w_down"], prep["final_g"], tm)


def kernel(x_prompt, x_sample, meta_tokens, norm_mix_g, w_in, q_norm_g, k_norm_g, conv_w, conv_b, lru_w_a, lru_b_a, lru_w_x, lru_b_x, lru_lam, attn_out_g, lru_out_g, w_out, norm_ffn_g, w_gate_up, w_down, final_norm_g):
    assert w_in.shape[0] == 1, "single-layer trunk"
    D = x_prompt.shape[-1]
    lw = D - ATTN_WIDTH
    qkv_w = ATTN_WIDTH + 2 * KV_WIDTH
    w = w_in[0]
    prep = {
        "mix_g": norm_mix_g[0][None],
        "wqkvT": w[:, :qkv_w].T.astype(BF16),
        "wlru": w[:, qkv_w:].astype(BF16),
        "qg": q_norm_g[0][:, None],
        "kg": k_norm_g[0][:, None],
        "conv_w": conv_w[0],
        "conv_b": conv_b[0][None],
        "lam": lru_lam[0],
        "attn_g": attn_out_g[0][:, None],
        "lru_g": lru_out_g[0][None],
        "wo_a": w_out[0][:ATTN_WIDTH].astype(BF16),
        "wo_l": w_out[0][ATTN_WIDTH:].astype(BF16),
        "ffn_g": norm_ffn_g[0][None],
        "w_gate": w_gate_up[0][:, :w_gate_up.shape[2] // 2].astype(BF16),
        "w_up": w_gate_up[0][:, w_gate_up.shape[2] // 2:].astype(BF16),
        "w_down": w_down[0].astype(BF16),
        "final_g": final_norm_g[None],
    }
    wa = [_block_diag_pairs(lru_w_a[0, d]) for d in range(2)]
    wx = [_block_diag_pairs(lru_w_x[0, d]) for d in range(2)]
    prep["w_gates"] = jnp.concatenate([wa[0], wx[0], wa[1], wx[1]], axis=2).astype(BF16)
    ncg = lw // LRU_LANES
    bias = [b.reshape(ncg, 1, LRU_LANES) for b in (lru_b_a[0, 0], lru_b_x[0, 0], lru_b_a[0, 1], lru_b_x[0, 1])]
    prep["b_gates"] = jnp.concatenate(bias, axis=2)

    meta = jnp.zeros((1, META_PAD, D), F32).at[0, :N_META].set(meta_tokens)
    ones = jnp.ones((HEAD_DIM, META_PAD), F32)
    _, k_m, vT_m, lin_m, _ = _inproj(meta, prep["mix_g"], prep["wqkvT"], prep["wlru"], prep["qg"], prep["kg"],
                                     ones, jnp.zeros_like(ones), META_PAD)
    prep["k_meta"] = k_m[0, :N_META]
    prep["vT_meta"] = vT_m[0, 0, :, :N_META]
    prep["lin_meta"] = lin_m[0, :N_META]
    return _trunk(x_prompt, prep), _trunk(x_sample, prep)
```

```python
import functools

import jax
import jax.numpy as jnp
import numpy as np
from jax import lax
from jax.experimental import pallas as pl
from jax.experimental.pallas import tpu as pltpu

F32 = jnp.float32
BF16 = jnp.bfloat16

N_META = 16
GRID_W = 64
HEAD_DIM = 64
N_Q_HEADS = 8
N_KV_HEADS = 2
Q_PER_KV = N_Q_HEADS // N_KV_HEADS
ATTN_WIDTH = N_Q_HEADS * HEAD_DIM
KV_WIDTH = N_KV_HEADS * HEAD_DIM
LRU_BLOCKS = 8
CONV_W = 4
CONV_LEFT = 2
LRU_C = 8.0
ROPE_AXIS_DIM = HEAD_DIM // 2
ROPE_THETA = 10000.0
EPS = 1e-6
LOG2_E = float(np.log2(np.e))

LANES = 128
SUBLANES = 8
BF16_ROWS = 16
VMEM_LIMIT_BYTES = 56 * 1024 * 1024

LRU_LANES = LANES
SCAN_SEGMENTS = SUBLANES
SCAN_UNROLL_MAX = 6
CONV_HALO = SUBLANES
KV_CHUNK = 256
CHUNKS_PER_STEP = 4
QK_AHEAD = 2
ACC_ROWS = HEAD_DIM + SUBLANES
META_PAD = KV_CHUNK


def _rms_rows(x, g):
    ms = jnp.mean(x * x, axis=-1, keepdims=True)
    return x * lax.rsqrt(ms + EPS) * g


def _head_norm_rope(xT, g_col, cos, sin_signed):
    nh = xT.shape[0] // HEAD_DIM
    x = xT.reshape(nh, HEAD_DIM, xT.shape[1])
    ms = jnp.mean(x * x, axis=1, keepdims=True)
    x = x * lax.rsqrt(ms + EPS) * g_col[None]
    q = ROPE_AXIS_DIM // 2
    partner = jnp.concatenate(
        [x[:, q:2 * q], x[:, 0:q], x[:, 3 * q:4 * q], x[:, 2 * q:3 * q]], axis=1)
    x = x * cos[None] + partner * sin_signed[None]
    return x.reshape(xT.shape)


def _inproj_kernel(x_ref, g_ref, wqkv_ref, wlru_ref, qg_ref, kg_ref, cos_ref, sin_ref,
                   qT_ref, k_ref, vT_ref, lin_ref, lgate_ref):
    x = x_ref[0]
    xn = _rms_rows(x, g_ref[...]).astype(BF16)
    qkvT = lax.dot_general(wqkv_ref[...], xn, (((1,), (1,)), ((), ())),
                           preferred_element_type=F32)
    lru = jnp.dot(xn, wlru_ref[...], preferred_element_type=F32)
    cos = cos_ref[...]
    sin = sin_ref[...]
    qT = _head_norm_rope(qkvT[:ATTN_WIDTH], qg_ref[...], cos, sin) * (HEAD_DIM ** -0.5 * LOG2_E)
    qT_ref[0] = qT.astype(BF16)
    kT = _head_norm_rope(qkvT[ATTN_WIDTH:ATTN_WIDTH + KV_WIDTH], kg_ref[...], cos, sin)
    k_ref[0] = kT.T.astype(BF16)
    vT = qkvT[ATTN_WIDTH + KV_WIDTH:].astype(BF16)
    for j in range(vT_ref.shape[1]):
        vT_ref[0, j] = vT[:, j * KV_CHUNK:(j + 1) * KV_CHUNK]
    w = lin_ref.shape[2]
    lin_ref[0] = lru[:, :w]
    lgate_ref[0] = lru[:, w:]


def _inproj(x, g, wqkvT, wlru, qg_col, kg_col, cosT, sinT, tm):
    B, S, D = x.shape
    lw = wlru.shape[1] // 2
    nt = S // tm
    const = lambda b, i: (0, 0)
    return pl.pallas_call(
        _inproj_kernel,
        grid=(B, nt),
        in_specs=[
            pl.BlockSpec((1, tm, D), lambda b, i: (b, i, 0)),
            pl.BlockSpec((1, D), const),
            pl.BlockSpec(wqkvT.shape, const),
            pl.BlockSpec(wlru.shape, const),
            pl.BlockSpec((HEAD_DIM, 1), const),
            pl.BlockSpec((HEAD_DIM, 1), const),
            pl.BlockSpec((HEAD_DIM, tm), lambda b, i: (0, i)),
            pl.BlockSpec((HEAD_DIM, tm), lambda b, i: (0, i)),
        ],
        out_specs=[
            pl.BlockSpec((1, ATTN_WIDTH, tm), lambda b, i: (b, 0, i)),
            pl.BlockSpec((1, tm, KV_WIDTH), lambda b, i: (b, i, 0)),
            pl.BlockSpec((1, tm // KV_CHUNK, KV_WIDTH, KV_CHUNK), lambda b, i: (b, i, 0, 0)),
            pl.BlockSpec((1, tm, lw), lambda b, i: (b, i, 0)),
            pl.BlockSpec((1, tm, lw), lambda b, i: (b, i, 0)),
        ],
        out_shape=[
            jax.ShapeDtypeStruct((B, ATTN_WIDTH, S), BF16),
            jax.ShapeDtypeStruct((B, S, KV_WIDTH), BF16),
            jax.ShapeDtypeStruct((B, S // KV_CHUNK, KV_WIDTH, KV_CHUNK), BF16),
            jax.ShapeDtypeStruct((B, S, lw), F32),
            jax.ShapeDtypeStruct((B, S, lw), F32),
        ],
        compiler_params=pltpu.CompilerParams(
            dimension_semantics=("parallel", "parallel"), vmem_limit_bytes=VMEM_LIMIT_BYTES),
        name="inproj",
    )(x, g, wqkvT, wlru, qg_col, kg_col, cosT, sinT)


def _gelu_tanh(x):
    return 0.5 * x * (1.0 + jnp.tanh(np.sqrt(2.0 / np.pi).astype(np.float32) * (x + 0.044715 * (x * x * x))))


def _lru_kernel(xm_ref, x_ref, gate_ref, cw_ref, cb_ref, wg_ref, bg_ref, lam_ref, y_ref,
                X, Af, Uf, Ab, Ub, *, L, T, TY, K):
    S = L - N_META
    H = CONV_HALO
    seg = L // SCAN_SEGMENTS
    zeros = jnp.zeros((H, LRU_LANES), F32)
    X[0:H] = zeros
    X[H + L:H + L + H] = zeros
    X[H:H + N_META] = xm_ref[...]
    X[H + N_META:H + L] = x_ref[0]

    lam = lam_ref[...]
    z = -lam
    softplus = jnp.maximum(z, 0.0) + jnp.log1p(jnp.exp(-jnp.abs(z)))
    half_decay = (0.5 * LRU_C) * softplus
    cw = cw_ref[...]
    cb = cb_ref[...]
    bg = bg_ref[...]

    def gate_chunk(i, carry):
        r0 = pl.multiple_of(i * T, SUBLANES)
        c = cb
        for j in range(CONV_W):
            c = c + X[pl.ds(r0 + (H - CONV_LEFT + j), T), :] * cw[j:j + 1]
        gh = jnp.dot(c.astype(BF16), wg_ref[...], preferred_element_type=F32) + bg
        for d, (A, U) in enumerate(((Af, Uf), (Ab, Ub))):
            base = 2 * d * LRU_LANES
            t_r = jnp.tanh(gh[:, base:base + LRU_LANES])
            t_i = jnp.tanh(gh[:, base + LRU_LANES:base + 2 * LRU_LANES])
            hd = half_decay[d:d + 1]
            nla = hd * t_r + hd
            a = jnp.exp2(nla * (-LOG2_E))
            x = jnp.tanh(nla) * (1.0 + a * a)
            mult = jnp.where(x == 0.0, 0.0, x * lax.rsqrt(x))
            A[pl.ds(r0, T), :] = a
            U[pl.ds(r0, T), :] = (mult * c) * (0.5 * t_i + 0.5)
        return carry

    lax.fori_loop(0, L // T, gate_chunk, 0)

    def seg_rows(j):
        return pl.ds(j, SCAN_SEGMENTS, stride=seg)

    def scan_block(jb, carry):
        hf, pf, hb, pb = carry
        j0 = jb * K
        fwd = [(Af[seg_rows(j0 + k), :], Uf[seg_rows(j0 + k), :]) for k in range(K)]
        bwd = [(Ab[seg_rows(seg - 1 - j0 - k), :], Ub[seg_rows(seg - 1 - j0 - k), :]) for k in range(K)]
        out = []
        for k in range(K):
            a, u = fwd[k]
            hf = a * hf + u
            pf = a * pf
            a, u = bwd[k]
            hb = a * hb + u
            pb = a * pb
            out.append((hf, pf, hb, pb))
        for k in range(K):
            Uf[seg_rows(j0 + k), :] = out[k][0]
            Af[seg_rows(j0 + k), :] = out[k][1]
            Ub[seg_rows(seg - 1 - j0 - k), :] = out[k][2]
            Ab[seg_rows(seg - 1 - j0 - k), :] = out[k][3]
        return hf, pf, hb, pb

    zero = jnp.zeros((SCAN_SEGMENTS, LRU_LANES), F32)
    one = jnp.ones((SCAN_SEGMENTS, LRU_LANES), F32)
    ef, pf, eb, pb = lax.fori_loop(0, seg // K, scan_block, (zero, one, zero, one))

    row = jnp.zeros((1, LRU_LANES), F32)
    rows = []
    for r in range(SCAN_SEGMENTS):
        rows.append(row)
        row = ef[r:r + 1] + pf[r:r + 1] * row
    cin_f = jnp.concatenate(rows, axis=0)
    row = jnp.zeros((1, LRU_LANES), F32)
    rows = []
    for r in reversed(range(SCAN_SEGMENTS)):
        rows.append(row)
        row = eb[r:r + 1] + pb[r:r + 1] * row
    cin_b = jnp.concatenate(rows[::-1], axis=0)

    def fix_block(jb, carry):
        j0 = jb * K
        h = [Uf[seg_rows(j0 + k), :] + Af[seg_rows(j0 + k), :] * cin_f
             + Ub[seg_rows(j0 + k), :] + Ab[seg_rows(j0 + k), :] * cin_b for k in range(K)]
        for k in range(K):
            Uf[seg_rows(j0 + k), :] = h[k]
        return carry

    lax.fori_loop(0, seg // K, fix_block, 0)

    def out_chunk(i, carry):
        r0 = pl.multiple_of(i * TY, SUBLANES)
        h = Uf[pl.ds(N_META + r0, TY), :]
        y_ref[0, pl.ds(r0, TY), :] = h * _gelu_tanh(gate_ref[0, pl.ds(r0, TY), :])
        return carry

    lax.fori_loop(0, S // TY, out_chunk, 0)


def _pick_chunk(n, align, cap):
    best = align
    for t in range(align, cap + 1, align):
        if n % t == 0:
            best = t
    return best


def _lru(xin_meta, xin, gate, cw, cb, wg, bg, lam):
    B, S, W = xin.shape
    L = S + N_META
    assert L % SCAN_SEGMENTS == 0 and W % LRU_LANES == 0
    ncg = W // LRU_LANES
    T = _pick_chunk(L, BF16_ROWS, 512)
    TY = _pick_chunk(S, SUBLANES, 512)
    assert L % T == 0 and S % TY == 0
    scratch_rows = L + 2 * CONV_HALO
    K = _pick_chunk(L // SCAN_SEGMENTS, 1, SCAN_UNROLL_MAX)
    kern = functools.partial(_lru_kernel, L=L, T=T, TY=TY, K=K)
    return pl.pallas_call(
        kern,
        grid=(B, ncg),
        in_specs=[
            pl.BlockSpec((N_META, LRU_LANES), lambda b, c: (0, c)),
            pl.BlockSpec((1, S, LRU_LANES), lambda b, c: (b, 0, c)),
            pl.BlockSpec((1, S, LRU_LANES), lambda b, c: (b, 0, c)),
            pl.BlockSpec((CONV_W, LRU_LANES), lambda b, c: (0, c)),
            pl.BlockSpec((1, LRU_LANES), lambda b, c: (0, c)),
            pl.BlockSpec((None, LRU_LANES, 4 * LRU_LANES), lambda b, c: (c, 0, 0)),
            pl.BlockSpec((None, 1, 4 * LRU_LANES), lambda b, c: (c, 0, 0)),
            pl.BlockSpec((2, LRU_LANES), lambda b, c: (0, c)),
        ],
        out_specs=pl.BlockSpec((1, S, LRU_LANES), lambda b, c: (b, 0, c)),
        out_shape=jax.ShapeDtypeStruct((B, S, W), F32),
        scratch_shapes=[pltpu.VMEM((scratch_rows, LRU_LANES), F32)]
        + [pltpu.VMEM((L, LRU_LANES), F32) for _ in range(4)],
        compiler_params=pltpu.CompilerParams(
            dimension_semantics=("parallel", "parallel"), vmem_limit_bytes=VMEM_LIMIT_BYTES),
        name="lru",
    )(xin_meta, xin, gate, cw, cb, wg, bg, lam)


def _attn_kernel(qT_ref, k_ref, vT_ref, km_ref, vTm_ref, og_ref, o_ref, qext, m_s, acc_s):
    tq = qT_ref.shape[2]
    nchunks = vT_ref.shape[1]
    zeros = jnp.zeros((HEAD_DIM, tq), BF16)
    for h in range(N_Q_HEADS):
        qh = qT_ref[0, h * HEAD_DIM:(h + 1) * HEAD_DIM, :]
        parts = [zeros] * N_KV_HEADS
        parts[h // Q_PER_KV] = qh
        qext[h] = jnp.concatenate(parts, axis=0)

    def with_ones(vT_all, g):
        ones = jnp.ones((BF16_ROWS, vT_all.shape[1]), BF16)
        return jnp.concatenate([vT_all[g * HEAD_DIM:(g + 1) * HEAD_DIM], ones], axis=0)

    km = km_ref[...]
    vTm = [with_ones(vTm_ref[...], g) for g in range(N_KV_HEADS)]
    for h in range(N_Q_HEADS):
        s = jnp.dot(km, qext[h], preferred_element_type=F32)
        m = jnp.max(s, axis=0, keepdims=True)
        p = jnp.exp2(s - m)
        m_s[h] = m
        acc_s[h] = jnp.dot(vTm[h // Q_PER_KV], p.astype(BF16),
                           preferred_element_type=F32)[:ACC_ROWS]

    def kv_step(i, carry):
        kcs, vcs = [], []
        for j in range(CHUNKS_PER_STEP):
            c = i * CHUNKS_PER_STEP + j
            kcs.append(k_ref[0, pl.ds(pl.multiple_of(c * KV_CHUNK, KV_CHUNK), KV_CHUNK), :])
            vcs.append([with_ones(vT_ref[0, c], g) for g in range(N_KV_HEADS)])
        units = [(j, h) for j in range(CHUNKS_PER_STEP) for h in range(N_Q_HEADS)]
        scores = lambda u: jnp.dot(kcs[u[0]], qext[u[1]], preferred_element_type=F32)
        pending = [scores(u) for u in units[:QK_AHEAD]]
        for n, (j, h) in enumerate(units):
            s = pending.pop(0)
            if n + QK_AHEAD < len(units):
                pending.append(scores(units[n + QK_AHEAD]))
            m_old = m_s[h]
            m_new = jnp.maximum(m_old, jnp.max(s, axis=0, keepdims=True))
            alpha = jnp.exp2(m_old - m_new)
            p = jnp.exp2(s - m_new)
            pv = jnp.dot(vcs[j][h // Q_PER_KV], p.astype(BF16), preferred_element_type=F32)
            acc_s[h] = alpha * acc_s[h] + pv[:ACC_ROWS]
            m_s[h] = m_new
        return carry

    assert nchunks % CHUNKS_PER_STEP == 0
    lax.fori_loop(0, nchunks // CHUNKS_PER_STEP, kv_step, 0)

    oT = jnp.concatenate([acc_s[h, :HEAD_DIM] / acc_s[h, HEAD_DIM:HEAD_DIM + 1]
                          for h in range(N_Q_HEADS)], axis=0)
    ms = jnp.mean(oT * oT, axis=0, keepdims=True)
    oT = oT * lax.rsqrt(ms + EPS) * og_ref[...]
    o_ref[0] = oT.T.astype(BF16)


def _attention(qT, k, vT, k_meta, vT_meta, og_col, tq):
    B, _, S = qT.shape
    nchunks = S // KV_CHUNK
    return pl.pallas_call(
        _attn_kernel,
        grid=(B, S // tq),
        in_specs=[
            pl.BlockSpec((1, ATTN_WIDTH, tq), lambda b, i: (b, 0, i)),
            pl.BlockSpec((1, S, KV_WIDTH), lambda b, i: (b, 0, 0)),
            pl.BlockSpec((1, nchunks, KV_WIDTH, KV_CHUNK), lambda b, i: (b, 0, 0, 0)),
            pl.BlockSpec((N_META, KV_WIDTH), lambda b, i: (0, 0)),
            pl.BlockSpec((KV_WIDTH, N_META), lambda b, i: (0, 0)),
            pl.BlockSpec((ATTN_WIDTH, 1), lambda b, i: (0, 0)),
        ],
        out_specs=pl.BlockSpec((1, tq, ATTN_WIDTH), lambda b, i: (b, i, 0)),
        out_shape=jax.ShapeDtypeStruct((B, S, ATTN_WIDTH), BF16),
        scratch_shapes=[
            pltpu.VMEM((N_Q_HEADS, KV_WIDTH, tq), BF16),
            pltpu.VMEM((N_Q_HEADS, 1, tq), F32),
            pltpu.VMEM((N_Q_HEADS, ACC_ROWS, tq), F32),
        ],
        compiler_params=pltpu.CompilerParams(
            dimension_semantics=("parallel", "parallel"), vmem_limit_bytes=VMEM_LIMIT_BYTES),
        name="attention",
    )(qT, k, vT, k_meta, vT_meta, og_col)


def _post_kernel(x_ref, ao_ref, ly_ref, lg_ref, woa_ref, wol_ref, fg_ref, wg_ref, wu_ref, wd_ref,
                 og_ref, o_ref, *, n_ffn_chunks):
    lo = _rms_rows(ly_ref[0], lg_ref[...]).astype(BF16)
    h = (x_ref[0]
         + jnp.dot(ao_ref[0], woa_ref[...], preferred_element_type=F32)
         + jnp.dot(lo, wol_ref[...], preferred_element_type=F32))
    xn = _rms_rows(h, fg_ref[...]).astype(BF16)
    fw = wg_ref.shape[1] // n_ffn_chunks
    for c in range(n_ffn_chunks):
        gate = jnp.dot(xn, wg_ref[:, c * fw:(c + 1) * fw], preferred_element_type=F32)
        up = jnp.dot(xn, wu_ref[:, c * fw:(c + 1) * fw], preferred_element_type=F32)
        act = (gate * jax.nn.sigmoid(gate) * up).astype(BF16)
        h = h + jnp.dot(act, wd_ref[c * fw:(c + 1) * fw, :], preferred_element_type=F32)
    o_ref[0] = _rms_rows(h, og_ref[...])


def _post(x, attn_o, lru_y, lru_g, wo_a, wo_l, ffn_g, w_gate, w_up, w_down, out_g, tm):
    B, S, D = x.shape
    F = w_gate.shape[1]
    n_ffn_chunks = 1
    const = lambda b, i: (0, 0)
    resident = functools.partial(pl.BlockSpec, index_map=const, pipeline_mode=pl.Buffered(1))
    row = lambda w: pl.BlockSpec((1, tm, w), lambda b, i: (b, i, 0))
    return pl.pallas_call(
        functools.partial(_post_kernel, n_ffn_chunks=n_ffn_chunks),
        grid=(B, S // tm),
        in_specs=[
            row(D), row(ATTN_WIDTH), row(lru_y.shape[2]),
            resident((1, lru_y.shape[2])),
            resident(wo_a.shape), resident(wo_l.shape),
            resident((1, D)),
            resident(w_gate.shape), resident(w_up.shape), resident(w_down.shape),
            resident((1, D)),
        ],
        out_specs=row(D),
        out_shape=jax.ShapeDtypeStruct((B, S, D), F32),
        compiler_params=pltpu.CompilerParams(
            dimension_semantics=("parallel", "parallel"), vmem_limit_bytes=VMEM_LIMIT_BYTES),
        name="post",
    )(x, attn_o, lru_y, lru_g, wo_a, wo_l, ffn_g, w_gate, w_up, w_down, out_g)


def _rope_tables(n_tokens):
    n_rows = n_tokens // GRID_W
    row = jnp.repeat(jnp.arange(n_rows), GRID_W).astype(F32)
    col = jnp.tile(jnp.arange(GRID_W), n_rows).astype(F32)
    freqs = ROPE_THETA ** (-jnp.arange(0, ROPE_AXIS_DIM, 2, dtype=F32) / ROPE_AXIS_DIM)
    ar = freqs[:, None] * row[None, :]
    ac = freqs[:, None] * col[None, :]
    cos = jnp.concatenate([jnp.cos(ar), jnp.cos(ar), jnp.cos(ac), jnp.cos(ac)], axis=0)
    sin = jnp.concatenate([-jnp.sin(ar), jnp.sin(ar), -jnp.sin(ac), jnp.sin(ac)], axis=0)
    return cos, sin


def _block_diag_pairs(w):
    nb, bw, _ = w.shape
    w = w.reshape(nb // 2, 2, bw, bw)
    z = jnp.zeros((nb // 2, bw, bw), w.dtype)
    top = jnp.concatenate([w[:, 0], z], axis=2)
    bot = jnp.concatenate([z, w[:, 1]], axis=2)
    return jnp.concatenate([top, bot], axis=1)


def _row_tile(S):
    return 512 if S % 512 == 0 else S


def _trunk(x, prep):
    B, S, D = x.shape
    tm = _row_tile(S)
    cosT, sinT = _rope_tables(S)
    qT, k, vT, lin, lgate = _inproj(x, prep["mix_g"], prep["wqkvT"], prep["wlru"], prep["qg"], prep["kg"],
                                    cosT, sinT, tm)
    lru_y = _lru(prep["lin_meta"], lin, lgate, prep["conv_w"], prep["conv_b"], prep["w_gates"],
                 prep["b_gates"], prep["lam"])
    attn_o = _attention(qT, k, vT, prep["k_meta"], prep["vT_meta"], prep["attn_g"], tm)
    return _post(x, attn_o, lru_y, prep["lru_g"], prep["wo_a"], prep["wo_l"], prep["ffn_g"],
                 prep["w_gate"], prep["w_up"], prep["w_down"], prep["final_g"], tm)


def kernel(x_prompt, x_sample, meta_tokens, norm_mix_g, w_in, q_norm_g, k_norm_g, conv_w, conv_b, lru_w_a, lru_b_a, lru_w_x, lru_b_x, lru_lam, attn_out_g, lru_out_g, w_out, norm_ffn_g, w_gate_up, w_down, final_norm_g):
    assert w_in.shape[0] == 1, "single-layer trunk"
    D = x_prompt.shape[-1]
    lw = D - ATTN_WIDTH
    qkv_w = ATTN_WIDTH + 2 * KV_WIDTH
    w = w_in[0]
    prep = {
        "mix_g": norm_mix_g[0][None],
        "wqkvT": w[:, :qkv_w].T.astype(BF16),
        "wlru": w[:, qkv_w:].astype(BF16),
        "qg": q_norm_g[0][:, None],
        "kg": k_norm_g[0][:, None],
        "conv_w": conv_w[0],
        "conv_b": conv_b[0][None],
        "lam": lru_lam[0],
        "attn_g": attn_out_g[0][:, None],
        "lru_g": lru_out_g[0][None],
        "wo_a": w_out[0][:ATTN_WIDTH].astype(BF16),
        "wo_l": w_out[0][ATTN_WIDTH:].astype(BF16),
        "ffn_g": norm_ffn_g[0][None],
        "w_gate": w_gate_up[0][:, :w_gate_up.shape[2] // 2].astype(BF16),
        "w_up": w_gate_up[0][:, w_gate_up.shape[2] // 2:].astype(BF16),
        "w_down": w_down[0].astype(BF16),
        "final_g": final_norm_g[None],
    }
    wa = [_block_diag_pairs(lru_w_a[0, d]) for d in range(2)]
    wx = [_block_diag_pairs(lru_w_x[0, d]) for d in range(2)]
    prep["w_gates"] = (0.5 * jnp.concatenate([wa[0], wx[0], wa[1], wx[1]], axis=2)).astype(BF16)
    ncg = lw // LRU_LANES
    bias = [b.reshape(ncg, 1, LRU_LANES) for b in (lru_b_a[0, 0], lru_b_x[0, 0], lru_b_a[0, 1], lru_b_x[0, 1])]
    prep["b_gates"] = 0.5 * jnp.concatenate(bias, axis=2)

    meta = jnp.zeros((1, META_PAD, D), F32).at[0, :N_META].set(meta_tokens)
    ones = jnp.ones((HEAD_DIM, META_PAD), F32)
    _, k_m, vT_m, lin_m, _ = _inproj(meta, prep["mix_g"], prep["wqkvT"], prep["wlru"], prep["qg"], prep["kg"],
                                     ones, jnp.zeros_like(ones), META_PAD)
    prep["k_meta"] = k_m[0, :N_META]
    prep["vT_meta"] = vT_m[0, 0, :, :N_META]
    prep["lin_meta"] = lin_m[0, :N_META]
    return _trunk(x_prompt, prep), _trunk(x_sample, prep)
```

```python
import functools

import jax
import jax.numpy as jnp
import numpy as np
from jax import lax
from jax.experimental import pallas as pl
from jax.experimental.pallas import tpu as pltpu

F32 = jnp.float32
BF16 = jnp.bfloat16

N_META = 16
GRID_W = 64
HEAD_DIM = 64
N_Q_HEADS = 8
N_KV_HEADS = 2
Q_PER_KV = N_Q_HEADS // N_KV_HEADS
ATTN_WIDTH = N_Q_HEADS * HEAD_DIM
KV_WIDTH = N_KV_HEADS * HEAD_DIM
LRU_BLOCKS = 8
CONV_W = 4
CONV_LEFT = 2
LRU_C = 8.0
ROPE_AXIS_DIM = HEAD_DIM // 2
ROPE_THETA = 10000.0
EPS = 1e-6
LOG2_E = float(np.log2(np.e))

LANES = 128
SUBLANES = 8
BF16_ROWS = 16
VMEM_LIMIT_BYTES = 56 * 1024 * 1024

LRU_LANES = LANES
SCAN_SEGMENTS = SUBLANES
SCAN_UNROLL_MAX = 6
CONV_HALO = SUBLANES
KV_CHUNK = 256
CHUNKS_PER_STEP = 4
QK_AHEAD = 2
ACC_ROWS = HEAD_DIM + SUBLANES
META_PAD = KV_CHUNK


def _rms_rows(x, g):
    ms = jnp.mean(x * x, axis=-1, keepdims=True)
    return x * lax.rsqrt(ms + EPS) * g


def _head_norm_rope(xT, g_col, cos, sin_signed):
    nh = xT.shape[0] // HEAD_DIM
    x = xT.reshape(nh, HEAD_DIM, xT.shape[1])
    ms = jnp.mean(x * x, axis=1, keepdims=True)
    x = x * lax.rsqrt(ms + EPS) * g_col[None]
    q = ROPE_AXIS_DIM // 2
    partner = jnp.concatenate(
        [x[:, q:2 * q], x[:, 0:q], x[:, 3 * q:4 * q], x[:, 2 * q:3 * q]], axis=1)
    x = x * cos[None] + partner * sin_signed[None]
    return x.reshape(xT.shape)


def _inproj_kernel(x_ref, g_ref, wqkv_ref, wlru_ref, qg_ref, kg_ref, cos_ref, sin_ref,
                   qT_ref, k_ref, vT_ref, lin_ref, lgate_ref):
    x = x_ref[0]
    xn = _rms_rows(x, g_ref[...]).astype(BF16)
    qkvT = lax.dot_general(wqkv_ref[...], xn, (((1,), (1,)), ((), ())),
                           preferred_element_type=F32)
    lru = jnp.dot(xn, wlru_ref[...], preferred_element_type=F32)
    cos = cos_ref[...]
    sin = sin_ref[...]
    qT = _head_norm_rope(qkvT[:ATTN_WIDTH], qg_ref[...], cos, sin) * (HEAD_DIM ** -0.5 * LOG2_E)
    qT_ref[0] = qT.astype(BF16)
    kT = _head_norm_rope(qkvT[ATTN_WIDTH:ATTN_WIDTH + KV_WIDTH], kg_ref[...], cos, sin)
    k_ref[0] = kT.T.astype(BF16)
    vT = qkvT[ATTN_WIDTH + KV_WIDTH:].astype(BF16)
    for j in range(vT_ref.shape[1]):
        vT_ref[0, j] = vT[:, j * KV_CHUNK:(j + 1) * KV_CHUNK]
    w = lin_ref.shape[2]
    lin_ref[0] = lru[:, :w]
    lgate_ref[0] = lru[:, w:]


def _inproj(x, g, wqkvT, wlru, qg_col, kg_col, cosT, sinT, tm):
    B, S, D = x.shape
    lw = wlru.shape[1] // 2
    nt = S // tm
    const = lambda b, i: (0, 0)
    return pl.pallas_call(
        _inproj_kernel,
        grid=(B, nt),
        in_specs=[
            pl.BlockSpec((1, tm, D), lambda b, i: (b, i, 0)),
            pl.BlockSpec((1, D), const),
            pl.BlockSpec(wqkvT.shape, const),
            pl.BlockSpec(wlru.shape, const),
            pl.BlockSpec((HEAD_DIM, 1), const),
            pl.BlockSpec((HEAD_DIM, 1), const),
            pl.BlockSpec((HEAD_DIM, tm), lambda b, i: (0, i)),
            pl.BlockSpec((HEAD_DIM, tm), lambda b, i: (0, i)),
        ],
        out_specs=[
            pl.BlockSpec((1, ATTN_WIDTH, tm), lambda b, i: (b, 0, i)),
            pl.BlockSpec((1, tm, KV_WIDTH), lambda b, i: (b, i, 0)),
            pl.BlockSpec((1, tm // KV_CHUNK, KV_WIDTH, KV_CHUNK), lambda b, i: (b, i, 0, 0)),
            pl.BlockSpec((1, tm, lw), lambda b, i: (b, i, 0)),
            pl.BlockSpec((1, tm, lw), lambda b, i: (b, i, 0)),
        ],
        out_shape=[
            jax.ShapeDtypeStruct((B, ATTN_WIDTH, S), BF16),
            jax.ShapeDtypeStruct((B, S, KV_WIDTH), BF16),
            jax.ShapeDtypeStruct((B, S // KV_CHUNK, KV_WIDTH, KV_CHUNK), BF16),
            jax.ShapeDtypeStruct((B, S, lw), F32),
            jax.ShapeDtypeStruct((B, S, lw), F32),
        ],
        compiler_params=pltpu.CompilerParams(
            dimension_semantics=("parallel", "parallel"), vmem_limit_bytes=VMEM_LIMIT_BYTES),
        name="inproj",
    )(x, g, wqkvT, wlru, qg_col, kg_col, cosT, sinT)


def _gelu_tanh(x):
    return 0.5 * x * (1.0 + jnp.tanh(np.sqrt(2.0 / np.pi).astype(np.float32) * (x + 0.044715 * (x * x * x))))


def _lru_kernel(xm_ref, x_ref, gate_ref, cw_ref, cb_ref, wg_ref, bg_ref, lam_ref, y_ref,
                X, Af, Uf, Ab, Ub, *, L, T, TY, K):
    S = L - N_META
    H = CONV_HALO
    seg = L // SCAN_SEGMENTS
    zeros = jnp.zeros((H, LRU_LANES), F32)
    X[0:H] = zeros
    X[H + L:H + L + H] = zeros
    X[H:H + N_META] = xm_ref[...]
    X[H + N_META:H + L] = x_ref[0]

    lam = lam_ref[...]
    z = -lam
    softplus = jnp.maximum(z, 0.0) + jnp.log1p(jnp.exp(-jnp.abs(z)))
    half_decay = (0.5 * LRU_C) * softplus
    cw = cw_ref[...]
    cb = cb_ref[...]
    bg = bg_ref[...]

    def gate_chunk(i, carry):
        r0 = pl.multiple_of(i * T, SUBLANES)
        c = cb
        for j in range(CONV_W):
            c = c + X[pl.ds(r0 + (H - CONV_LEFT + j), T), :] * cw[j:j + 1]
        gh = jnp.dot(c.astype(BF16), wg_ref[...], preferred_element_type=F32) + bg
        for d, (A, U) in enumerate(((Af, Uf), (Ab, Ub))):
            base = 2 * d * LRU_LANES
            t_r = jnp.tanh(gh[:, base:base + LRU_LANES])
            t_i = jnp.tanh(gh[:, base + LRU_LANES:base + 2 * LRU_LANES])
            hd = half_decay[d:d + 1]
            nla = hd * t_r + hd
            a = jnp.exp2(nla * (-LOG2_E))
            x = jnp.tanh(nla) * (1.0 + a * a)
            mult = jnp.where(x == 0.0, 0.0, x * lax.rsqrt(x))
            A[pl.ds(r0, T), :] = a
            U[pl.ds(r0, T), :] = (mult * c) * (0.5 * t_i + 0.5)
        return carry

    lax.fori_loop(0, L // T, gate_chunk, 0)

    def seg_rows(j):
        return pl.ds(j, SCAN_SEGMENTS, stride=seg)

    def scan_block(jb, carry):
        hf, pf, hb, pb = carry
        j0 = jb * K
        fwd = [(Af[seg_rows(j0 + k), :], Uf[seg_rows(j0 + k), :]) for k in range(K)]
        bwd = [(Ab[seg_rows(seg - 1 - j0 - k), :], Ub[seg_rows(seg - 1 - j0 - k), :]) for k in range(K)]
        out = []
        for k in range(K):
            a, u = fwd[k]
            hf = a * hf + u
            pf = a * pf
            a, u = bwd[k]
            hb = a * hb + u
            pb = a * pb
            out.append((hf, pf, hb, pb))
        for k in range(K):
            Uf[seg_rows(j0 + k), :] = out[k][0]
            Af[seg_rows(j0 + k), :] = out[k][1]
            Ub[seg_rows(seg - 1 - j0 - k), :] = out[k][2]
            Ab[seg_rows(seg - 1 - j0 - k), :] = out[k][3]
        return hf, pf, hb, pb

    zero = jnp.zeros((SCAN_SEGMENTS, LRU_LANES), F32)
    one = jnp.ones((SCAN_SEGMENTS, LRU_LANES), F32)
    ef, pf, eb, pb = lax.fori_loop(0, seg // K, scan_block, (zero, one, zero, one))

    row = jnp.zeros((1, LRU_LANES), F32)
    rows = []
    for r in range(SCAN_SEGMENTS):
        rows.append(row)
        row = ef[r:r + 1] + pf[r:r + 1] * row
    cin_f = jnp.concatenate(rows, axis=0)
    row = jnp.zeros((1, LRU_LANES), F32)
    rows = []
    for r in reversed(range(SCAN_SEGMENTS)):
        rows.append(row)
        row = eb[r:r + 1] + pb[r:r + 1] * row
    cin_b = jnp.concatenate(rows[::-1], axis=0)

    def fix_block(jb, carry):
        j0 = jb * K
        h = [Uf[seg_rows(j0 + k), :] + Af[seg_rows(j0 + k), :] * cin_f
             + Ub[seg_rows(j0 + k), :] + Ab[seg_rows(j0 + k), :] * cin_b for k in range(K)]
        for k in range(K):
            Uf[seg_rows(j0 + k), :] = h[k]
        return carry

    lax.fori_loop(0, seg // K, fix_block, 0)

    def out_chunk(i, carry):
        r0 = pl.multiple_of(i * TY, SUBLANES)
        h = Uf[pl.ds(N_META + r0, TY), :]
        y_ref[0, pl.ds(r0, TY), :] = h * _gelu_tanh(gate_ref[0, pl.ds(r0, TY), :])
        return carry

    lax.fori_loop(0, S // TY, out_chunk, 0)


def _pick_chunk(n, align, cap):
    best = align
    for t in range(align, cap + 1, align):
        if n % t == 0:
            best = t
    return best


def _lru(xin_meta, xin, gate, cw, cb, wg, bg, lam):
    B, S, W = xin.shape
    L = S + N_META
    assert L % SCAN_SEGMENTS == 0 and W % LRU_LANES == 0
    ncg = W // LRU_LANES
    T = _pick_chunk(L, BF16_ROWS, 512)
    TY = _pick_chunk(S, SUBLANES, 512)
    assert L % T == 0 and S % TY == 0
    scratch_rows = L + 2 * CONV_HALO
    K = _pick_chunk(L // SCAN_SEGMENTS, 1, SCAN_UNROLL_MAX)
    kern = functools.partial(_lru_kernel, L=L, T=T, TY=TY, K=K)
    return pl.pallas_call(
        kern,
        grid=(B, ncg),
        in_specs=[
            pl.BlockSpec((N_META, LRU_LANES), lambda b, c: (0, c)),
            pl.BlockSpec((1, S, LRU_LANES), lambda b, c: (b, 0, c)),
            pl.BlockSpec((1, S, LRU_LANES), lambda b, c: (b, 0, c)),
            pl.BlockSpec((CONV_W, LRU_LANES), lambda b, c: (0, c)),
            pl.BlockSpec((1, LRU_LANES), lambda b, c: (0, c)),
            pl.BlockSpec((None, LRU_LANES, 4 * LRU_LANES), lambda b, c: (c, 0, 0)),
            pl.BlockSpec((None, 1, 4 * LRU_LANES), lambda b, c: (c, 0, 0)),
            pl.BlockSpec((2, LRU_LANES), lambda b, c: (0, c)),
        ],
        out_specs=pl.BlockSpec((1, S, LRU_LANES), lambda b, c: (b, 0, c)),
        out_shape=jax.ShapeDtypeStruct((B, S, W), F32),
        scratch_shapes=[pltpu.VMEM((scratch_rows, LRU_LANES), F32)]
        + [pltpu.VMEM((L, LRU_LANES), F32) for _ in range(4)],
        compiler_params=pltpu.CompilerParams(
            dimension_semantics=("parallel", "parallel"), vmem_limit_bytes=VMEM_LIMIT_BYTES),
        name="lru",
    )(xin_meta, xin, gate, cw, cb, wg, bg, lam)


def _attn_kernel(qT_ref, k_ref, vT_ref, km_ref, vTm_ref, og_ref, o_ref, qext, m_s, acc_s, s_carry):
    tq = qT_ref.shape[2]
    nchunks = vT_ref.shape[1]
    zeros = jnp.zeros((HEAD_DIM, tq), BF16)
    for h in range(N_Q_HEADS):
        qh = qT_ref[0, h * HEAD_DIM:(h + 1) * HEAD_DIM, :]
        parts = [zeros] * N_KV_HEADS
        parts[h // Q_PER_KV] = qh
        qext[h] = jnp.concatenate(parts, axis=0)

    def with_ones(vT_all, g):
        ones = jnp.ones((BF16_ROWS, vT_all.shape[1]), BF16)
        return jnp.concatenate([vT_all[g * HEAD_DIM:(g + 1) * HEAD_DIM], ones], axis=0)

    def scores(c, h):
        kc = k_ref[0, pl.ds(pl.multiple_of(c * KV_CHUNK, KV_CHUNK), KV_CHUNK), :]
        return jnp.dot(kc, qext[h], preferred_element_type=F32)

    for n in range(QK_AHEAD):
        s_carry[n] = scores(n // N_Q_HEADS, n % N_Q_HEADS)

    km = km_ref[...]
    vTm = [with_ones(vTm_ref[...], g) for g in range(N_KV_HEADS)]
    s_meta = [jnp.dot(km, qext[h], preferred_element_type=F32) for h in range(N_Q_HEADS)]
    p_meta = []
    for h in range(N_Q_HEADS):
        m = jnp.max(s_meta[h], axis=0, keepdims=True)
        m_s[h] = m
        p_meta.append(jnp.exp2(s_meta[h] - m).astype(BF16))
    for h in range(N_Q_HEADS):
        acc_s[h] = jnp.dot(vTm[h // Q_PER_KV], p_meta[h], preferred_element_type=F32)[:ACC_ROWS]

    def kv_step(i, carry):
        vcs = []
        for j in range(CHUNKS_PER_STEP):
            c = i * CHUNKS_PER_STEP + j
            vcs.append([with_ones(vT_ref[0, c], g) for g in range(N_KV_HEADS)])
        units = [(j, h) for j in range(CHUNKS_PER_STEP) for h in range(N_Q_HEADS)]
        pending = [s_carry[n] for n in range(QK_AHEAD)]
        for n, (j, h) in enumerate(units):
            s = pending.pop(0)
            if n + QK_AHEAD < len(units):
                pending.append(scores(i * CHUNKS_PER_STEP + units[n + QK_AHEAD][0], units[n + QK_AHEAD][1]))
            else:
                nxt = n + QK_AHEAD - len(units)
                c_next = jnp.minimum((i + 1) * CHUNKS_PER_STEP + units[nxt][0], nchunks - 1)
                s_carry[nxt] = scores(c_next, units[nxt][1])
            m_old = m_s[h]
            m_new = jnp.maximum(m_old, jnp.max(s, axis=0, keepdims=True))
            alpha = jnp.exp2(m_old - m_new)
            p = jnp.exp2(s - m_new)
            pv = jnp.dot(vcs[j][h // Q_PER_KV], p.astype(BF16), preferred_element_type=F32)
            acc_s[h] = alpha * acc_s[h] + pv[:ACC_ROWS]
            m_s[h] = m_new
        return carry

    assert nchunks % CHUNKS_PER_STEP == 0
    lax.fori_loop(0, nchunks // CHUNKS_PER_STEP, kv_step, 0)

    oT = jnp.concatenate([acc_s[h, :HEAD_DIM] / acc_s[h, HEAD_DIM:HEAD_DIM + 1]
                          for h in range(N_Q_HEADS)], axis=0)
    ms = jnp.mean(oT * oT, axis=0, keepdims=True)
    oT = oT * lax.rsqrt(ms + EPS) * og_ref[...]
    o_ref[0] = oT.T.astype(BF16)


def _attention(qT, k, vT, k_meta, vT_meta, og_col, tq):
    B, _, S = qT.shape
    nchunks = S // KV_CHUNK
    return pl.pallas_call(
        _attn_kernel,
        grid=(B, S // tq),
        in_specs=[
            pl.BlockSpec((1, ATTN_WIDTH, tq), lambda b, i: (b, 0, i)),
            pl.BlockSpec((1, S, KV_WIDTH), lambda b, i: (b, 0, 0)),
            pl.BlockSpec((1, nchunks, KV_WIDTH, KV_CHUNK), lambda b, i: (b, 0, 0, 0)),
            pl.BlockSpec((N_META, KV_WIDTH), lambda b, i: (0, 0)),
            pl.BlockSpec((KV_WIDTH, N_META), lambda b, i: (0, 0)),
            pl.BlockSpec((ATTN_WIDTH, 1), lambda b, i: (0, 0)),
        ],
        out_specs=pl.BlockSpec((1, tq, ATTN_WIDTH), lambda b, i: (b, i, 0)),
        out_shape=jax.ShapeDtypeStruct((B, S, ATTN_WIDTH), BF16),
        scratch_shapes=[
            pltpu.VMEM((N_Q_HEADS, KV_WIDTH, tq), BF16),
            pltpu.VMEM((N_Q_HEADS, 1, tq), F32),
            pltpu.VMEM((N_Q_HEADS, ACC_ROWS, tq), F32),
            pltpu.VMEM((QK_AHEAD, KV_CHUNK, tq), F32),
        ],
        compiler_params=pltpu.CompilerParams(
            dimension_semantics=("parallel", "parallel"), vmem_limit_bytes=VMEM_LIMIT_BYTES),
        name="attention",
    )(qT, k, vT, k_meta, vT_meta, og_col)


def _post_kernel(x_ref, ao_ref, ly_ref, lg_ref, woa_ref, wol_ref, fg_ref, wg_ref, wu_ref, wd_ref,
                 og_ref, o_ref, *, n_ffn_chunks):
    lo = _rms_rows(ly_ref[0], lg_ref[...]).astype(BF16)
    h = (x_ref[0]
         + jnp.dot(ao_ref[0], woa_ref[...], preferred_element_type=F32)
         + jnp.dot(lo, wol_ref[...], preferred_element_type=F32))
    xn = _rms_rows(h, fg_ref[...]).astype(BF16)
    fw = wg_ref.shape[1] // n_ffn_chunks
    for c in range(n_ffn_chunks):
        gate = jnp.dot(xn, wg_ref[:, c * fw:(c + 1) * fw], preferred_element_type=F32)
        up = jnp.dot(xn, wu_ref[:, c * fw:(c + 1) * fw], preferred_element_type=F32)
        act = (gate * jax.nn.sigmoid(gate) * up).astype(BF16)
        h = h + jnp.dot(act, wd_ref[c * fw:(c + 1) * fw, :], preferred_element_type=F32)
    o_ref[0] = _rms_rows(h, og_ref[...])


def _post(x, attn_o, lru_y, lru_g, wo_a, wo_l, ffn_g, w_gate, w_up, w_down, out_g, tm):
    B, S, D = x.shape
    F = w_gate.shape[1]
    n_ffn_chunks = 1
    const = lambda b, i: (0, 0)
    resident = functools.partial(pl.BlockSpec, index_map=const, pipeline_mode=pl.Buffered(1))
    row = lambda w: pl.BlockSpec((1, tm, w), lambda b, i: (b, i, 0))
    return pl.pallas_call(
        functools.partial(_post_kernel, n_ffn_chunks=n_ffn_chunks),
        grid=(B, S // tm),
        in_specs=[
            row(D), row(ATTN_WIDTH), row(lru_y.shape[2]),
            resident((1, lru_y.shape[2])),
            resident(wo_a.shape), resident(wo_l.shape),
            resident((1, D)),
            resident(w_gate.shape), resident(w_up.shape), resident(w_down.shape),
            resident((1, D)),
        ],
        out_specs=row(D),
        out_shape=jax.ShapeDtypeStruct((B, S, D), F32),
        compiler_params=pltpu.CompilerParams(
            dimension_semantics=("parallel", "parallel"), vmem_limit_bytes=VMEM_LIMIT_BYTES),
        name="post",
    )(x, attn_o, lru_y, lru_g, wo_a, wo_l, ffn_g, w_gate, w_up, w_down, out_g)


def _rope_tables(n_tokens):
    n_rows = n_tokens // GRID_W
    row = jnp.repeat(jnp.arange(n_rows), GRID_W).astype(F32)
    col = jnp.tile(jnp.arange(GRID_W), n_rows).astype(F32)
    freqs = ROPE_THETA ** (-jnp.arange(0, ROPE_AXIS_DIM, 2, dtype=F32) / ROPE_AXIS_DIM)
    ar = freqs[:, None] * row[None, :]
    ac = freqs[:, None] * col[None, :]
    cos = jnp.concatenate([jnp.cos(ar), jnp.cos(ar), jnp.cos(ac), jnp.cos(ac)], axis=0)
    sin = jnp.concatenate([-jnp.sin(ar), jnp.sin(ar), -jnp.sin(ac), jnp.sin(ac)], axis=0)
    return cos, sin


def _block_diag_pairs(w):
    nb, bw, _ = w.shape
    w = w.reshape(nb // 2, 2, bw, bw)
    z = jnp.zeros((nb // 2, bw, bw), w.dtype)
    top = jnp.concatenate([w[:, 0], z], axis=2)
    bot = jnp.concatenate([z, w[:, 1]], axis=2)
    return jnp.concatenate([top, bot], axis=1)


def _row_tile(S):
    return 512 if S % 512 == 0 else S


def _trunk(x, prep):
    B, S, D = x.shape
    tm = _row_tile(S)
    cosT, sinT = _rope_tables(S)
    qT, k, vT, lin, lgate = _inproj(x, prep["mix_g"], prep["wqkvT"], prep["wlru"], prep["qg"], prep["kg"],
                                    cosT, sinT, tm)
    lru_y = _lru(prep["lin_meta"], lin, lgate, prep["conv_w"], prep["conv_b"], prep["w_gates"],
                 prep["b_gates"], prep["lam"])
    attn_o = _attention(qT, k, vT, prep["k_meta"], prep["vT_meta"], prep["attn_g"], tm)
    return _post(x, attn_o, lru_y, prep["lru_g"], prep["wo_a"], prep["wo_l"], prep["ffn_g"],
                 prep["w_gate"], prep["w_up"], prep["w_down"], prep["final_g"], tm)


def kernel(x_prompt, x_sample, meta_tokens, norm_mix_g, w_in, q_norm_g, k_norm_g, conv_w, conv_b, lru_w_a, lru_b_a, lru_w_x, lru_b_x, lru_lam, attn_out_g, lru_out_g, w_out, norm_ffn_g, w_gate_up, w_down, final_norm_g):
    assert w_in.shape[0] == 1, "single-layer trunk"
    D = x_prompt.shape[-1]
    lw = D - ATTN_WIDTH
    qkv_w = ATTN_WIDTH + 2 * KV_WIDTH
    w = w_in[0]
    prep = {
        "mix_g": norm_mix_g[0][None],
        "wqkvT": w[:, :qkv_w].T.astype(BF16),
        "wlru": w[:, qkv_w:].astype(BF16),
        "qg": q_norm_g[0][:, None],
        "kg": k_norm_g[0][:, None],
        "conv_w": conv_w[0],
        "conv_b": conv_b[0][None],
        "lam": lru_lam[0],
        "attn_g": attn_out_g[0][:, None],
        "lru_g": lru_out_g[0][None],
        "wo_a": w_out[0][:ATTN_WIDTH].astype(BF16),
        "wo_l": w_out[0][ATTN_WIDTH:].astype(BF16),
        "ffn_g": norm_ffn_g[0][None],
        "w_gate": w_gate_up[0][:, :w_gate_up.shape[2] // 2].astype(BF16),
        "w_up": w_gate_up[0][:, w_gate_up.shape[2] // 2:].astype(BF16),
        "w_down": w_down[0].astype(BF16),
        "final_g": final_norm_g[None],
    }
    wa = [_block_diag_pairs(lru_w_a[0, d]) for d in range(2)]
    wx = [_block_diag_pairs(lru_w_x[0, d]) for d in range(2)]
    prep["w_gates"] = (0.5 * jnp.concatenate([wa[0], wx[0], wa[1], wx[1]], axis=2)).astype(BF16)
    ncg = lw // LRU_LANES
    bias = [b.reshape(ncg, 1, LRU_LANES) for b in (lru_b_a[0, 0], lru_b_x[0, 0], lru_b_a[0, 1], lru_b_x[0, 1])]
    prep["b_gates"] = 0.5 * jnp.concatenate(bias, axis=2)

    meta = jnp.zeros((1, META_PAD, D), F32).at[0, :N_META].set(meta_tokens)
    ones = jnp.ones((HEAD_DIM, META_PAD), F32)
    _, k_m, vT_m, lin_m, _ = _inproj(meta, prep["mix_g"], prep["wqkvT"], prep["wlru"], prep["qg"], prep["kg"],
                                     ones, jnp.zeros_like(ones), META_PAD)
    prep["k_meta"] = k_m[0, :N_META]
    prep["vT_meta"] = vT_m[0, 0, :, :N_META]
    prep["lin_meta"] = lin_m[0, :N_META]
    return _trunk(x_prompt, prep), _trunk(x_sample, prep)
```

```python
import functools

import jax
import jax.numpy as jnp
import numpy as np
from jax import lax
from jax.experimental import pallas as pl
from jax.experimental.pallas import tpu as pltpu

F32 = jnp.float32
BF16 = jnp.bfloat16

N_META = 16
GRID_W = 64
HEAD_DIM = 64
N_Q_HEADS = 8
N_KV_HEADS = 2
Q_PER_KV = N_Q_HEADS // N_KV_HEADS
ATTN_WIDTH = N_Q_HEADS * HEAD_DIM
KV_WIDTH = N_KV_HEADS * HEAD_DIM
LRU_BLOCKS = 8
CONV_W = 4
CONV_LEFT = 2
LRU_C = 8.0
ROPE_AXIS_DIM = HEAD_DIM // 2
ROPE_THETA = 10000.0
EPS = 1e-6
LOG2_E = float(np.log2(np.e))

LANES = 128
SUBLANES = 8
BF16_ROWS = 16
VMEM_LIMIT_BYTES = 56 * 1024 * 1024

LRU_LANES = LANES
SCAN_SEGMENTS = SUBLANES
SCAN_UNROLL_MAX = 6
CONV_HALO = SUBLANES
KV_CHUNK = 256
CHUNKS_PER_STEP = 4
QK_AHEAD = 2
SAFE_EXCESS = 64.0
ACC_ROWS = HEAD_DIM + SUBLANES
META_PAD = KV_CHUNK


def _rms_rows(x, g):
    ms = jnp.mean(x * x, axis=-1, keepdims=True)
    return x * lax.rsqrt(ms + EPS) * g


def _head_norm_rope(xT, g_col, cos, sin_signed):
    nh = xT.shape[0] // HEAD_DIM
    x = xT.reshape(nh, HEAD_DIM, xT.shape[1])
    ms = jnp.mean(x * x, axis=1, keepdims=True)
    x = x * lax.rsqrt(ms + EPS) * g_col[None]
    q = ROPE_AXIS_DIM // 2
    partner = jnp.concatenate(
        [x[:, q:2 * q], x[:, 0:q], x[:, 3 * q:4 * q], x[:, 2 * q:3 * q]], axis=1)
    x = x * cos[None] + partner * sin_signed[None]
    return x.reshape(xT.shape)


def _inproj_kernel(x_ref, g_ref, wqkv_ref, wlru_ref, qg_ref, kg_ref, cos_ref, sin_ref,
                   qT_ref, k_ref, vT_ref, lin_ref, lgate_ref):
    x = x_ref[0]
    xn = _rms_rows(x, g_ref[...]).astype(BF16)
    qkvT = lax.dot_general(wqkv_ref[...], xn, (((1,), (1,)), ((), ())),
                           preferred_element_type=F32)
    lru = jnp.dot(xn, wlru_ref[...], preferred_element_type=F32)
    cos = cos_ref[...]
    sin = sin_ref[...]
    qT = _head_norm_rope(qkvT[:ATTN_WIDTH], qg_ref[...], cos, sin) * (HEAD_DIM ** -0.5 * LOG2_E)
    qT_ref[0] = qT.astype(BF16)
    kT = _head_norm_rope(qkvT[ATTN_WIDTH:ATTN_WIDTH + KV_WIDTH], kg_ref[...], cos, sin)
    k_ref[0] = kT.T.astype(BF16)
    vT = qkvT[ATTN_WIDTH + KV_WIDTH:].astype(BF16)
    for j in range(vT_ref.shape[1]):
        vT_ref[0, j] = vT[:, j * KV_CHUNK:(j + 1) * KV_CHUNK]
    w = lin_ref.shape[2]
    lin_ref[0] = lru[:, :w]
    lgate_ref[0] = lru[:, w:]


def _inproj(x, g, wqkvT, wlru, qg_col, kg_col, cosT, sinT, tm):
    B, S, D = x.shape
    lw = wlru.shape[1] // 2
    nt = S // tm
    const = lambda b, i: (0, 0)
    return pl.pallas_call(
        _inproj_kernel,
        grid=(B, nt),
        in_specs=[
            pl.BlockSpec((1, tm, D), lambda b, i: (b, i, 0)),
            pl.BlockSpec((1, D), const),
            pl.BlockSpec(wqkvT.shape, const),
            pl.BlockSpec(wlru.shape, const),
            pl.BlockSpec((HEAD_DIM, 1), const),
            pl.BlockSpec((HEAD_DIM, 1), const),
            pl.BlockSpec((HEAD_DIM, tm), lambda b, i: (0, i)),
            pl.BlockSpec((HEAD_DIM, tm), lambda b, i: (0, i)),
        ],
        out_specs=[
            pl.BlockSpec((1, ATTN_WIDTH, tm), lambda b, i: (b, 0, i)),
            pl.BlockSpec((1, tm, KV_WIDTH), lambda b, i: (b, i, 0)),
            pl.BlockSpec((1, tm // KV_CHUNK, KV_WIDTH, KV_CHUNK), lambda b, i: (b, i, 0, 0)),
            pl.BlockSpec((1, tm, lw), lambda b, i: (b, i, 0)),
            pl.BlockSpec((1, tm, lw), lambda b, i: (b, i, 0)),
        ],
        out_shape=[
            jax.ShapeDtypeStruct((B, ATTN_WIDTH, S), BF16),
            jax.ShapeDtypeStruct((B, S, KV_WIDTH), BF16),
            jax.ShapeDtypeStruct((B, S // KV_CHUNK, KV_WIDTH, KV_CHUNK), BF16),
            jax.ShapeDtypeStruct((B, S, lw), F32),
            jax.ShapeDtypeStruct((B, S, lw), F32),
        ],
        compiler_params=pltpu.CompilerParams(
            dimension_semantics=("parallel", "parallel"), vmem_limit_bytes=VMEM_LIMIT_BYTES),
        name="inproj",
    )(x, g, wqkvT, wlru, qg_col, kg_col, cosT, sinT)


def _gelu_tanh(x):
    return 0.5 * x * (1.0 + jnp.tanh(np.sqrt(2.0 / np.pi).astype(np.float32) * (x + 0.044715 * (x * x * x))))


def _lru_kernel(xm_ref, x_ref, gate_ref, cw_ref, cb_ref, wg_ref, bg_ref, lam_ref, y_ref,
                X, Af, Uf, Ab, Ub, *, L, T, TY, K):
    S = L - N_META
    H = CONV_HALO
    seg = L // SCAN_SEGMENTS
    zeros = jnp.zeros((H, LRU_LANES), F32)
    X[0:H] = zeros
    X[H + L:H + L + H] = zeros
    X[H:H + N_META] = xm_ref[...]
    X[H + N_META:H + L] = x_ref[0]

    lam = lam_ref[...]
    z = -lam
    softplus = jnp.maximum(z, 0.0) + jnp.log1p(jnp.exp(-jnp.abs(z)))
    half_decay = (0.5 * LRU_C) * softplus
    cw = cw_ref[...]
    cb = cb_ref[...]
    bg = bg_ref[...]

    def gate_chunk(i, carry):
        r0 = pl.multiple_of(i * T, SUBLANES)
        c = cb
        for j in range(CONV_W):
            c = c + X[pl.ds(r0 + (H - CONV_LEFT + j), T), :] * cw[j:j + 1]
        gh = jnp.dot(c.astype(BF16), wg_ref[...], preferred_element_type=F32) + bg
        for d, (A, U) in enumerate(((Af, Uf), (Ab, Ub))):
            base = 2 * d * LRU_LANES
            t_r = jnp.tanh(gh[:, base:base + LRU_LANES])
            t_i = jnp.tanh(gh[:, base + LRU_LANES:base + 2 * LRU_LANES])
            hd = half_decay[d:d + 1]
            nla = hd * t_r + hd
            a = jnp.exp2(nla * (-LOG2_E))
            x = jnp.tanh(nla) * (1.0 + a * a)
            mult = jnp.where(x == 0.0, 0.0, x * lax.rsqrt(x))
            A[pl.ds(r0, T), :] = a
            U[pl.ds(r0, T), :] = (mult * c) * (0.5 * t_i + 0.5)
        return carry

    lax.fori_loop(0, L // T, gate_chunk, 0)

    def seg_rows(j):
        return pl.ds(j, SCAN_SEGMENTS, stride=seg)

    def scan_block(jb, carry):
        hf, pf, hb, pb = carry
        j0 = jb * K
        fwd = [(Af[seg_rows(j0 + k), :], Uf[seg_rows(j0 + k), :]) for k in range(K)]
        bwd = [(Ab[seg_rows(seg - 1 - j0 - k), :], Ub[seg_rows(seg - 1 - j0 - k), :]) for k in range(K)]
        out = []
        for k in range(K):
            a, u = fwd[k]
            hf = a * hf + u
            pf = a * pf
            a, u = bwd[k]
            hb = a * hb + u
            pb = a * pb
            out.append((hf, pf, hb, pb))
        for k in range(K):
            Uf[seg_rows(j0 + k), :] = out[k][0]
            Af[seg_rows(j0 + k), :] = out[k][1]
            Ub[seg_rows(seg - 1 - j0 - k), :] = out[k][2]
            Ab[seg_rows(seg - 1 - j0 - k), :] = out[k][3]
        return hf, pf, hb, pb

    zero = jnp.zeros((SCAN_SEGMENTS, LRU_LANES), F32)
    one = jnp.ones((SCAN_SEGMENTS, LRU_LANES), F32)
    ef, pf, eb, pb = lax.fori_loop(0, seg // K, scan_block, (zero, one, zero, one))

    row = jnp.zeros((1, LRU_LANES), F32)
    rows = []
    for r in range(SCAN_SEGMENTS):
        rows.append(row)
        row = ef[r:r + 1] + pf[r:r + 1] * row
    cin_f = jnp.concatenate(rows, axis=0)
    row = jnp.zeros((1, LRU_LANES), F32)
    rows = []
    for r in reversed(range(SCAN_SEGMENTS)):
        rows.append(row)
        row = eb[r:r + 1] + pb[r:r + 1] * row
    cin_b = jnp.concatenate(rows[::-1], axis=0)

    def fix_block(jb, carry):
        j0 = jb * K
        h = [Uf[seg_rows(j0 + k), :] + Af[seg_rows(j0 + k), :] * cin_f
             + Ub[seg_rows(j0 + k), :] + Ab[seg_rows(j0 + k), :] * cin_b for k in range(K)]
        for k in range(K):
            Uf[seg_rows(j0 + k), :] = h[k]
        return carry

    lax.fori_loop(0, seg // K, fix_block, 0)

    def out_chunk(i, carry):
        r0 = pl.multiple_of(i * TY, SUBLANES)
        h = Uf[pl.ds(N_META + r0, TY), :]
        y_ref[0, pl.ds(r0, TY), :] = h * _gelu_tanh(gate_ref[0, pl.ds(r0, TY), :])
        return carry

    lax.fori_loop(0, S // TY, out_chunk, 0)


def _pick_chunk(n, align, cap):
    best = align
    for t in range(align, cap + 1, align):
        if n % t == 0:
            best = t
    return best


def _lru(xin_meta, xin, gate, cw, cb, wg, bg, lam):
    B, S, W = xin.shape
    L = S + N_META
    assert L % SCAN_SEGMENTS == 0 and W % LRU_LANES == 0
    ncg = W // LRU_LANES
    T = _pick_chunk(L, BF16_ROWS, 512)
    TY = _pick_chunk(S, SUBLANES, 512)
    assert L % T == 0 and S % TY == 0
    scratch_rows = L + 2 * CONV_HALO
    K = _pick_chunk(L // SCAN_SEGMENTS, 1, SCAN_UNROLL_MAX)
    kern = functools.partial(_lru_kernel, L=L, T=T, TY=TY, K=K)
    return pl.pallas_call(
        kern,
        grid=(B, ncg),
        in_specs=[
            pl.BlockSpec((N_META, LRU_LANES), lambda b, c: (0, c)),
            pl.BlockSpec((1, S, LRU_LANES), lambda b, c: (b, 0, c)),
            pl.BlockSpec((1, S, LRU_LANES), lambda b, c: (b, 0, c)),
            pl.BlockSpec((CONV_W, LRU_LANES), lambda b, c: (0, c)),
            pl.BlockSpec((1, LRU_LANES), lambda b, c: (0, c)),
            pl.BlockSpec((None, LRU_LANES, 4 * LRU_LANES), lambda b, c: (c, 0, 0)),
            pl.BlockSpec((None, 1, 4 * LRU_LANES), lambda b, c: (c, 0, 0)),
            pl.BlockSpec((2, LRU_LANES), lambda b, c: (0, c)),
        ],
        out_specs=pl.BlockSpec((1, S, LRU_LANES), lambda b, c: (b, 0, c)),
        out_shape=jax.ShapeDtypeStruct((B, S, W), F32),
        scratch_shapes=[pltpu.VMEM((scratch_rows, LRU_LANES), F32)]
        + [pltpu.VMEM((L, LRU_LANES), F32) for _ in range(4)],
        compiler_params=pltpu.CompilerParams(
            dimension_semantics=("parallel", "parallel"), vmem_limit_bytes=VMEM_LIMIT_BYTES),
        name="lru",
    )(xin_meta, xin, gate, cw, cb, wg, bg, lam)


def _attn_kernel(qT_ref, k_ref, vT_ref, km_ref, vTm_ref, og_ref, o_ref,
                 qext, m_s, acc_s, s_carry, smax_carry):
    tq = qT_ref.shape[2]
    nchunks = vT_ref.shape[1]
    zeros = jnp.zeros((HEAD_DIM, tq), BF16)
    for h in range(N_Q_HEADS):
        qh = qT_ref[0, h * HEAD_DIM:(h + 1) * HEAD_DIM, :]
        parts = [zeros] * N_KV_HEADS
        parts[h // Q_PER_KV] = qh
        qext[h] = jnp.concatenate(parts, axis=0)

    def with_ones(vT_all, g):
        ones = jnp.ones((BF16_ROWS, vT_all.shape[1]), BF16)
        return jnp.concatenate([vT_all[g * HEAD_DIM:(g + 1) * HEAD_DIM], ones], axis=0)

    def scores(c, h):
        kc = k_ref[0, pl.ds(pl.multiple_of(c * KV_CHUNK, KV_CHUNK), KV_CHUNK), :]
        return jnp.dot(kc, qext[h], preferred_element_type=F32)

    for n in range(QK_AHEAD):
        first = scores(n // N_Q_HEADS, n % N_Q_HEADS)
        s_carry[n] = first
        smax_carry[n] = jnp.max(first, axis=0, keepdims=True)

    km = km_ref[...]
    vTm = [with_ones(vTm_ref[...], g) for g in range(N_KV_HEADS)]
    s_meta = [jnp.dot(km, qext[h], preferred_element_type=F32) for h in range(N_Q_HEADS)]
    p_meta = []
    for h in range(N_Q_HEADS):
        m = jnp.max(s_meta[h], axis=0, keepdims=True)
        m_s[0, h] = m
        p_meta.append(jnp.exp2(s_meta[h] - m).astype(BF16))
    for h in range(N_Q_HEADS):
        acc_s[0, h] = jnp.dot(vTm[h // Q_PER_KV], p_meta[h], preferred_element_type=F32)[:ACC_ROWS]

    def values(c):
        return [with_ones(vT_ref[0, c], g) for g in range(N_KV_HEADS)]

    def exact_trip(i, src, dst):
        def chunk(j, carry):
            c = i * CHUNKS_PER_STEP + j
            rd = jnp.where(j == 0, src, dst)
            vc = values(c)
            for h in range(N_Q_HEADS):
                s = scores(c, h)
                m_old = m_s[rd, h]
                m_new = jnp.maximum(m_old, jnp.max(s, axis=0, keepdims=True))
                p = jnp.exp2(s - m_new)
                pv = jnp.dot(vc[h // Q_PER_KV], p.astype(BF16), preferred_element_type=F32)
                acc_s[dst, h] = jnp.exp2(m_old - m_new) * acc_s[rd, h] + pv[:ACC_ROWS]
                m_s[dst, h] = m_new
            return carry
        lax.fori_loop(0, CHUNKS_PER_STEP, chunk, 0)

    def fast_trip(i, src, dst):
        vcs = [values(i * CHUNKS_PER_STEP + j) for j in range(CHUNKS_PER_STEP)]
        units = [(j, h) for j in range(CHUNKS_PER_STEP) for h in range(N_Q_HEADS)]
        m_start = [m_s[src, h] for h in range(N_Q_HEADS)]
        pending = [(s_carry[n], smax_carry[n]) for n in range(QK_AHEAD)]
        excess = jnp.full((1, tq), -jnp.inf, F32)
        for n in range(QK_AHEAD):
            excess = jnp.maximum(excess, pending[n][1] - m_start[units[n][1]])
        for n, (j, h) in enumerate(units):
            rd = src if j == 0 else dst
            s, s_max = pending.pop(0)
            if n + QK_AHEAD < len(units):
                ja, ha = units[n + QK_AHEAD]
                ahead = scores(i * CHUNKS_PER_STEP + ja, ha)
                ahead_max = jnp.max(ahead, axis=0, keepdims=True)
                excess = jnp.maximum(excess, ahead_max - m_start[ha])
                pending.append((ahead, ahead_max))
            else:
                nxt = n + QK_AHEAD - len(units)
                c_next = jnp.minimum((i + 1) * CHUNKS_PER_STEP + units[nxt][0], nchunks - 1)
                ahead = scores(c_next, units[nxt][1])
                s_carry[nxt] = ahead
                smax_carry[nxt] = jnp.max(ahead, axis=0, keepdims=True)
            m_ref = m_s[rd, h]
            p = jnp.exp2(s - m_ref)
            pv = jnp.dot(vcs[j][h // Q_PER_KV], p.astype(BF16), preferred_element_type=F32)
            m_new = jnp.maximum(m_ref, s_max)
            acc_s[dst, h] = (acc_s[rd, h] + pv[:ACC_ROWS]) * jnp.exp2(m_ref - m_new)
            m_s[dst, h] = m_new

        @pl.when(jnp.logical_not(jnp.max(excess) <= SAFE_EXCESS))
        def _():
            exact_trip(i, src, dst)

    def kv_step(i2, carry):
        fast_trip(2 * i2, 0, 1)
        fast_trip(2 * i2 + 1, 1, 0)
        return carry

    assert nchunks % (2 * CHUNKS_PER_STEP) == 0
    lax.fori_loop(0, nchunks // (2 * CHUNKS_PER_STEP), kv_step, 0)

    oT = jnp.concatenate([acc_s[0, h, :HEAD_DIM] / acc_s[0, h, HEAD_DIM:HEAD_DIM + 1]
                          for h in range(N_Q_HEADS)], axis=0)
    ms = jnp.mean(oT * oT, axis=0, keepdims=True)
    oT = oT * lax.rsqrt(ms + EPS) * og_ref[...]
    o_ref[0] = oT.T.astype(BF16)


def _attention(qT, k, vT, k_meta, vT_meta, og_col, tq):
    B, _, S = qT.shape
    nchunks = S // KV_CHUNK
    return pl.pallas_call(
        _attn_kernel,
        grid=(B, S // tq),
        in_specs=[
            pl.BlockSpec((1, ATTN_WIDTH, tq), lambda b, i: (b, 0, i)),
            pl.BlockSpec((1, S, KV_WIDTH), lambda b, i: (b, 0, 0)),
            pl.BlockSpec((1, nchunks, KV_WIDTH, KV_CHUNK), lambda b, i: (b, 0, 0, 0)),
            pl.BlockSpec((N_META, KV_WIDTH), lambda b, i: (0, 0)),
            pl.BlockSpec((KV_WIDTH, N_META), lambda b, i: (0, 0)),
            pl.BlockSpec((ATTN_WIDTH, 1), lambda b, i: (0, 0)),
        ],
        out_specs=pl.BlockSpec((1, tq, ATTN_WIDTH), lambda b, i: (b, i, 0)),
        out_shape=jax.ShapeDtypeStruct((B, S, ATTN_WIDTH), BF16),
        scratch_shapes=[
            pltpu.VMEM((N_Q_HEADS, KV_WIDTH, tq), BF16),
            pltpu.VMEM((2, N_Q_HEADS, 1, tq), F32),
            pltpu.VMEM((2, N_Q_HEADS, ACC_ROWS, tq), F32),
            pltpu.VMEM((QK_AHEAD, KV_CHUNK, tq), F32),
            pltpu.VMEM((QK_AHEAD, 1, tq), F32),
        ],
        compiler_params=pltpu.CompilerParams(
            dimension_semantics=("parallel", "parallel"), vmem_limit_bytes=VMEM_LIMIT_BYTES),
        name="attention",
    )(qT, k, vT, k_meta, vT_meta, og_col)


def _post_kernel(x_ref, ao_ref, ly_ref, lg_ref, woa_ref, wol_ref, fg_ref, wg_ref, wu_ref, wd_ref,
                 og_ref, o_ref, *, n_ffn_chunks):
    lo = _rms_rows(ly_ref[0], lg_ref[...]).astype(BF16)
    h = (x_ref[0]
         + jnp.dot(ao_ref[0], woa_ref[...], preferred_element_type=F32)
         + jnp.dot(lo, wol_ref[...], preferred_element_type=F32))
    xn = _rms_rows(h, fg_ref[...]).astype(BF16)
    fw = wg_ref.shape[1] // n_ffn_chunks
    for c in range(n_ffn_chunks):
        gate = jnp.dot(xn, wg_ref[:, c * fw:(c + 1) * fw], preferred_element_type=F32)
        up = jnp.dot(xn, wu_ref[:, c * fw:(c + 1) * fw], preferred_element_type=F32)
        act = (gate * jax.nn.sigmoid(gate) * up).astype(BF16)
        h = h + jnp.dot(act, wd_ref[c * fw:(c + 1) * fw, :], preferred_element_type=F32)
    o_ref[0] = _rms_rows(h, og_ref[...])


def _post(x, attn_o, lru_y, lru_g, wo_a, wo_l, ffn_g, w_gate, w_up, w_down, out_g, tm):
    B, S, D = x.shape
    F = w_gate.shape[1]
    n_ffn_chunks = 1
    const = lambda b, i: (0, 0)
    resident = functools.partial(pl.BlockSpec, index_map=const, pipeline_mode=pl.Buffered(1))
    row = lambda w: pl.BlockSpec((1, tm, w), lambda b, i: (b, i, 0))
    return pl.pallas_call(
        functools.partial(_post_kernel, n_ffn_chunks=n_ffn_chunks),
        grid=(B, S // tm),
        in_specs=[
            row(D), row(ATTN_WIDTH), row(lru_y.shape[2]),
            resident((1, lru_y.shape[2])),
            resident(wo_a.shape), resident(wo_l.shape),
            resident((1, D)),
            resident(w_gate.shape), resident(w_up.shape), resident(w_down.shape),
            resident((1, D)),
        ],
        out_specs=row(D),
        out_shape=jax.ShapeDtypeStruct((B, S, D), F32),
        compiler_params=pltpu.CompilerParams(
            dimension_semantics=("parallel", "parallel"), vmem_limit_bytes=VMEM_LIMIT_BYTES),
        name="post",
    )(x, attn_o, lru_y, lru_g, wo_a, wo_l, ffn_g, w_gate, w_up, w_down, out_g)


def _rope_tables(n_tokens):
    n_rows = n_tokens // GRID_W
    row = jnp.repeat(jnp.arange(n_rows), GRID_W).astype(F32)
    col = jnp.tile(jnp.arange(GRID_W), n_rows).astype(F32)
    freqs = ROPE_THETA ** (-jnp.arange(0, ROPE_AXIS_DIM, 2, dtype=F32) / ROPE_AXIS_DIM)
    ar = freqs[:, None] * row[None, :]
    ac = freqs[:, None] * col[None, :]
    cos = jnp.concatenate([jnp.cos(ar), jnp.cos(ar), jnp.cos(ac), jnp.cos(ac)], axis=0)
    sin = jnp.concatenate([-jnp.sin(ar), jnp.sin(ar), -jnp.sin(ac), jnp.sin(ac)], axis=0)
    return cos, sin


def _block_diag_pairs(w):
    nb, bw, _ = w.shape
    w = w.reshape(nb // 2, 2, bw, bw)
    z = jnp.zeros((nb // 2, bw, bw), w.dtype)
    top = jnp.concatenate([w[:, 0], z], axis=2)
    bot = jnp.concatenate([z, w[:, 1]], axis=2)
    return jnp.concatenate([top, bot], axis=1)


def _row_tile(S):
    return 512 if S % 512 == 0 else S


def _trunk(x, prep):
    B, S, D = x.shape
    tm = _row_tile(S)
    cosT, sinT = _rope_tables(S)
    qT, k, vT, lin, lgate = _inproj(x, prep["mix_g"], prep["wqkvT"], prep["wlru"], prep["qg"], prep["kg"],
                                    cosT, sinT, tm)
    lru_y = _lru(prep["lin_meta"], lin, lgate, prep["conv_w"], prep["conv_b"], prep["w_gates"],
                 prep["b_gates"], prep["lam"])
    attn_o = _attention(qT, k, vT, prep["k_meta"], prep["vT_meta"], prep["attn_g"], tm)
    return _post(x, attn_o, lru_y, prep["lru_g"], prep["wo_a"], prep["wo_l"], prep["ffn_g"],
                 prep["w_gate"], prep["w_up"], prep["w_down"], prep["final_g"], tm)


def kernel(x_prompt, x_sample, meta_tokens, norm_mix_g, w_in, q_norm_g, k_norm_g, conv_w, conv_b, lru_w_a, lru_b_a, lru_w_x, lru_b_x, lru_lam, attn_out_g, lru_out_g, w_out, norm_ffn_g, w_gate_up, w_down, final_norm_g):
    assert w_in.shape[0] == 1, "single-layer trunk"
    D = x_prompt.shape[-1]
    lw = D - ATTN_WIDTH
    qkv_w = ATTN_WIDTH + 2 * KV_WIDTH
    w = w_in[0]
    prep = {
        "mix_g": norm_mix_g[0][None],
        "wqkvT": w[:, :qkv_w].T.astype(BF16),
        "wlru": w[:, qkv_w:].astype(BF16),
        "qg": q_norm_g[0][:, None],
        "kg": k_norm_g[0][:, None],
        "conv_w": conv_w[0],
        "conv_b": conv_b[0][None],
        "lam": lru_lam[0],
        "attn_g": attn_out_g[0][:, None],
        "lru_g": lru_out_g[0][None],
        "wo_a": w_out[0][:ATTN_WIDTH].astype(BF16),
        "wo_l": w_out[0][ATTN_WIDTH:].astype(BF16),
        "ffn_g": norm_ffn_g[0][None],
        "w_gate": w_gate_up[0][:, :w_gate_up.shape[2] // 2].astype(BF16),
        "w_up": w_gate_up[0][:, w_gate_up.shape[2] // 2:].astype(BF16),
        "w_down": w_down[0].astype(BF16),
        "final_g": final_norm_g[None],
    }
    wa = [_block_diag_pairs(lru_w_a[0, d]) for d in range(2)]
    wx = [_block_diag_pairs(lru_w_x[0, d]) for d in range(2)]
    prep["w_gates"] = (0.5 * jnp.concatenate([wa[0], wx[0], wa[1], wx[1]], axis=2)).astype(BF16)
    ncg = lw // LRU_LANES
    bias = [b.reshape(ncg, 1, LRU_LANES) for b in (lru_b_a[0, 0], lru_b_x[0, 0], lru_b_a[0, 1], lru_b_x[0, 1])]
    prep["b_gates"] = 0.5 * jnp.concatenate(bias, axis=2)

    meta = jnp.zeros((1, META_PAD, D), F32).at[0, :N_META].set(meta_tokens)
    ones = jnp.ones((HEAD_DIM, META_PAD), F32)
    _, k_m, vT_m, lin_m, _ = _inproj(meta, prep["mix_g"], prep["wqkvT"], prep["wlru"], prep["qg"], prep["kg"],
                                     ones, jnp.zeros_like(ones), META_PAD)
    prep["k_meta"] = k_m[0, :N_META]
    prep["vT_meta"] = vT_m[0, 0, :, :N_META]
    prep["lin_meta"] = lin_m[0, :N_META]
    return _trunk(x_prompt, prep), _trunk(x_sample, prep)
```

```python
import functools

import jax
import jax.numpy as jnp
import numpy as np
from jax import lax
from jax.experimental import pallas as pl
from jax.experimental.pallas import tpu as pltpu

F32 = jnp.float32
BF16 = jnp.bfloat16

N_META = 16
GRID_W = 64
HEAD_DIM = 64
N_Q_HEADS = 8
N_KV_HEADS = 2
Q_PER_KV = N_Q_HEADS // N_KV_HEADS
ATTN_WIDTH = N_Q_HEADS * HEAD_DIM
KV_WIDTH = N_KV_HEADS * HEAD_DIM
LRU_BLOCKS = 8
CONV_W = 4
CONV_LEFT = 2
LRU_C = 8.0
ROPE_AXIS_DIM = HEAD_DIM // 2
ROPE_THETA = 10000.0
EPS = 1e-6
LOG2_E = float(np.log2(np.e))

LANES = 128
SUBLANES = 8
BF16_ROWS = 16
VMEM_LIMIT_BYTES = 56 * 1024 * 1024

LRU_LANES = LANES
SCAN_SEGMENTS = SUBLANES
SCAN_GROUPS = 2
SCAN_UNROLL_MAX = 3
CONV_HALO = SUBLANES
KV_CHUNK = 256
CHUNKS_PER_STEP = 4
QK_AHEAD = 2
SAFE_EXCESS = 64.0
ACC_ROWS = HEAD_DIM + SUBLANES
META_PAD = KV_CHUNK


def _rms_rows(x, g):
    ms = jnp.mean(x * x, axis=-1, keepdims=True)
    return x * lax.rsqrt(ms + EPS) * g


def _head_norm_rope(xT, g_col, cos, sin_signed):
    nh = xT.shape[0] // HEAD_DIM
    x = xT.reshape(nh, HEAD_DIM, xT.shape[1])
    ms = jnp.mean(x * x, axis=1, keepdims=True)
    x = x * lax.rsqrt(ms + EPS) * g_col[None]
    q = ROPE_AXIS_DIM // 2
    partner = jnp.concatenate(
        [x[:, q:2 * q], x[:, 0:q], x[:, 3 * q:4 * q], x[:, 2 * q:3 * q]], axis=1)
    x = x * cos[None] + partner * sin_signed[None]
    return x.reshape(xT.shape)


def _inproj_kernel(x_ref, g_ref, wqkv_ref, wlru_ref, qg_ref, kg_ref, cos_ref, sin_ref,
                   qT_ref, k_ref, vT_ref, lin_ref, lgate_ref):
    w = lin_ref.shape[2]

    def project(j):
        rows = slice(j * KV_CHUNK, (j + 1) * KV_CHUNK)
        xn = _rms_rows(x_ref[0, rows, :], g_ref[...]).astype(BF16)
        qkvT = lax.dot_general(wqkv_ref[...], xn, (((1,), (1,)), ((), ())),
                               preferred_element_type=F32)
        lru = jnp.dot(xn, wlru_ref[...], preferred_element_type=F32)
        lin_ref[0, rows, :] = lru[:, :w]
        lgate_ref[0, rows, :] = lru[:, w:]
        return qkvT

    def finish(j, qkvT):
        rows = slice(j * KV_CHUNK, (j + 1) * KV_CHUNK)
        cos = cos_ref[:, rows]
        sin = sin_ref[:, rows]
        qT = _head_norm_rope(qkvT[:ATTN_WIDTH], qg_ref[...], cos, sin) * (HEAD_DIM ** -0.5 * LOG2_E)
        qT_ref[0, :, rows] = qT.astype(BF16)
        kT = _head_norm_rope(qkvT[ATTN_WIDTH:ATTN_WIDTH + KV_WIDTH], kg_ref[...], cos, sin)
        k_ref[0, rows, :] = kT.T.astype(BF16)
        vT_ref[0, j] = qkvT[ATTN_WIDTH + KV_WIDTH:].astype(BF16)

    pieces = vT_ref.shape[1]
    pending = project(0)
    for j in range(pieces):
        current = pending
        if j + 1 < pieces:
            pending = project(j + 1)
        finish(j, current)


def _inproj(x, g, wqkvT, wlru, qg_col, kg_col, cosT, sinT, tm):
    B, S, D = x.shape
    lw = wlru.shape[1] // 2
    nt = S // tm
    const = lambda b, i: (0, 0)
    return pl.pallas_call(
        _inproj_kernel,
        grid=(B, nt),
        in_specs=[
            pl.BlockSpec((1, tm, D), lambda b, i: (b, i, 0)),
            pl.BlockSpec((1, D), const),
            pl.BlockSpec(wqkvT.shape, const),
            pl.BlockSpec(wlru.shape, const),
            pl.BlockSpec((HEAD_DIM, 1), const),
            pl.BlockSpec((HEAD_DIM, 1), const),
            pl.BlockSpec((HEAD_DIM, tm), lambda b, i: (0, i)),
            pl.BlockSpec((HEAD_DIM, tm), lambda b, i: (0, i)),
        ],
        out_specs=[
            pl.BlockSpec((1, ATTN_WIDTH, tm), lambda b, i: (b, 0, i)),
            pl.BlockSpec((1, tm, KV_WIDTH), lambda b, i: (b, i, 0)),
            pl.BlockSpec((1, tm // KV_CHUNK, KV_WIDTH, KV_CHUNK), lambda b, i: (b, i, 0, 0)),
            pl.BlockSpec((1, tm, lw), lambda b, i: (b, i, 0)),
            pl.BlockSpec((1, tm, lw), lambda b, i: (b, i, 0)),
        ],
        out_shape=[
            jax.ShapeDtypeStruct((B, ATTN_WIDTH, S), BF16),
            jax.ShapeDtypeStruct((B, S, KV_WIDTH), BF16),
            jax.ShapeDtypeStruct((B, S // KV_CHUNK, KV_WIDTH, KV_CHUNK), BF16),
            jax.ShapeDtypeStruct((B, S, lw), F32),
            jax.ShapeDtypeStruct((B, S, lw), F32),
        ],
        compiler_params=pltpu.CompilerParams(
            dimension_semantics=("parallel", "parallel"), vmem_limit_bytes=VMEM_LIMIT_BYTES),
        name="inproj",
    )(x, g, wqkvT, wlru, qg_col, kg_col, cosT, sinT)


def _gelu_tanh(x):
    return 0.5 * x * (1.0 + jnp.tanh(np.sqrt(2.0 / np.pi).astype(np.float32) * (x + 0.044715 * (x * x * x))))


def _lru_kernel(xm_ref, x_ref, cw_ref, cb_ref, wg_ref, bg_ref, lam_ref, y_ref,
                X, Af, Uf, Ab, Ub, *, L, T, TY, K):
    S = L - N_META
    H = CONV_HALO
    seg = L // (SCAN_SEGMENTS * SCAN_GROUPS)
    zeros = jnp.zeros((H, LRU_LANES), F32)
    X[0:H] = zeros
    X[H + L:H + L + H] = zeros
    X[H:H + N_META] = xm_ref[...]
    X[H + N_META:H + L] = x_ref[0]

    lam = lam_ref[...]
    z = -lam
    softplus = jnp.maximum(z, 0.0) + jnp.log1p(jnp.exp(-jnp.abs(z)))
    half_decay = (0.5 * LRU_C) * softplus
    cw = cw_ref[...]
    cb = cb_ref[...]
    bg = bg_ref[...]

    def gate_chunk(i, carry):
        r0 = pl.multiple_of(i * T, SUBLANES)
        c = cb
        for j in range(CONV_W):
            c = c + X[pl.ds(r0 + (H - CONV_LEFT + j), T), :] * cw[j:j + 1]
        gh = jnp.dot(c.astype(BF16), wg_ref[...], preferred_element_type=F32) + bg
        for d, (A, U) in enumerate(((Af, Uf), (Ab, Ub))):
            base = 2 * d * LRU_LANES
            t_r = jnp.tanh(gh[:, base:base + LRU_LANES])
            t_i = jnp.tanh(gh[:, base + LRU_LANES:base + 2 * LRU_LANES])
            hd = half_decay[d:d + 1]
            nla = hd * t_r + hd
            a = jnp.exp2(nla * (-LOG2_E))
            x = jnp.tanh(nla) * (1.0 + a * a)
            mult = jnp.where(x == 0.0, 0.0, x * lax.rsqrt(x))
            A[pl.ds(r0, T), :] = a
            U[pl.ds(r0, T), :] = (mult * c) * (0.5 * t_i + 0.5)
        return carry

    lax.fori_loop(0, L // T, gate_chunk, 0)

    G = SCAN_GROUPS

    def seg_rows(g, j):
        return pl.ds(g * SCAN_SEGMENTS * seg + j, SCAN_SEGMENTS, stride=seg)

    def scan_block(jb, carry):
        state = [list(c) for c in carry]
        j0 = jb * K
        fwd = [[(Af[seg_rows(g, j0 + k), :], Uf[seg_rows(g, j0 + k), :]) for k in range(K)] for g in range(G)]
        bwd = [[(Ab[seg_rows(g, seg - 1 - j0 - k), :], Ub[seg_rows(g, seg - 1 - j0 - k), :])
                for k in range(K)] for g in range(G)]
        out = []
        for k in range(K):
            for g in range(G):
                hf, pf, hb, pb = state[g]
                a, u = fwd[g][k]
                hf = a * hf + u
                pf = a * pf
                a, u = bwd[g][k]
                hb = a * hb + u
                pb = a * pb
                state[g] = [hf, pf, hb, pb]
                out.append((g, k, hf, pf, hb, pb))
        for g, k, hf, pf, hb, pb in out:
            Uf[seg_rows(g, j0 + k), :] = hf
            Af[seg_rows(g, j0 + k), :] = pf
            Ub[seg_rows(g, seg - 1 - j0 - k), :] = hb
            Ab[seg_rows(g, seg - 1 - j0 - k), :] = pb
        return tuple(tuple(c) for c in state)

    zero = jnp.zeros((SCAN_SEGMENTS, LRU_LANES), F32)
    one = jnp.ones((SCAN_SEGMENTS, LRU_LANES), F32)
    ends = lax.fori_loop(0, seg // K, scan_block, tuple((zero, one, zero, one) for _ in range(G)))

    order = [(g, r) for g in range(G) for r in range(SCAN_SEGMENTS)]
    row = jnp.zeros((1, LRU_LANES), F32)
    rows = {}
    for g, r in order:
        rows[g, r] = row
        row = ends[g][0][r:r + 1] + ends[g][1][r:r + 1] * row
    cin_f = [jnp.concatenate([rows[g, r] for r in range(SCAN_SEGMENTS)], axis=0) for g in range(G)]
    row = jnp.zeros((1, LRU_LANES), F32)
    rows = {}
    for g, r in reversed(order):
        rows[g, r] = row
        row = ends[g][2][r:r + 1] + ends[g][3][r:r + 1] * row
    cin_b = [jnp.concatenate([rows[g, r] for r in range(SCAN_SEGMENTS)], axis=0) for g in range(G)]

    def fix_block(jb, carry):
        j0 = jb * K
        h = [(g, k, Uf[seg_rows(g, j0 + k), :] + Af[seg_rows(g, j0 + k), :] * cin_f[g]
              + Ub[seg_rows(g, j0 + k), :] + Ab[seg_rows(g, j0 + k), :] * cin_b[g])
             for k in range(K) for g in range(G)]
        for g, k, v in h:
            Uf[seg_rows(g, j0 + k), :] = v
        return carry

    lax.fori_loop(0, seg // K, fix_block, 0)

    def out_chunk(i, carry):
        r0 = pl.multiple_of(i * TY, SUBLANES)
        y_ref[0, pl.ds(r0, TY), :] = Uf[pl.ds(N_META + r0, TY), :]
        return carry

    lax.fori_loop(0, S // TY, out_chunk, 0)


def _pick_chunk(n, align, cap):
    best = align
    for t in range(align, cap + 1, align):
        if n % t == 0:
            best = t
    return best


def _lru(xin_meta, xin, cw, cb, wg, bg, lam):
    B, S, W = xin.shape
    L = S + N_META
    assert L % (SCAN_SEGMENTS * SCAN_GROUPS) == 0 and W % LRU_LANES == 0
    ncg = W // LRU_LANES
    T = _pick_chunk(L, BF16_ROWS, 512)
    TY = _pick_chunk(S, SUBLANES, 512)
    assert L % T == 0 and S % TY == 0
    scratch_rows = L + 2 * CONV_HALO
    K = _pick_chunk(L // (SCAN_SEGMENTS * SCAN_GROUPS), 1, SCAN_UNROLL_MAX)
    kern = functools.partial(_lru_kernel, L=L, T=T, TY=TY, K=K)
    return pl.pallas_call(
        kern,
        grid=(B, ncg),
        in_specs=[
            pl.BlockSpec((N_META, LRU_LANES), lambda b, c: (0, c)),
            pl.BlockSpec((1, S, LRU_LANES), lambda b, c: (b, 0, c)),
            pl.BlockSpec((CONV_W, LRU_LANES), lambda b, c: (0, c)),
            pl.BlockSpec((1, LRU_LANES), lambda b, c: (0, c)),
            pl.BlockSpec((None, LRU_LANES, 4 * LRU_LANES), lambda b, c: (c, 0, 0)),
            pl.BlockSpec((None, 1, 4 * LRU_LANES), lambda b, c: (c, 0, 0)),
            pl.BlockSpec((2, LRU_LANES), lambda b, c: (0, c)),
        ],
        out_specs=pl.BlockSpec((1, S, LRU_LANES), lambda b, c: (b, 0, c)),
        out_shape=jax.ShapeDtypeStruct((B, S, W), F32),
        scratch_shapes=[pltpu.VMEM((scratch_rows, LRU_LANES), F32)]
        + [pltpu.VMEM((L, LRU_LANES), F32) for _ in range(4)],
        compiler_params=pltpu.CompilerParams(
            dimension_semantics=("parallel", "parallel"), vmem_limit_bytes=VMEM_LIMIT_BYTES),
        name="lru",
    )(xin_meta, xin, cw, cb, wg, bg, lam)


def _attn_kernel(qT_ref, k_ref, vT_ref, km_ref, vTm_ref, og_ref, o_ref,
                 qext, m_s, acc_s, s_carry, smax_carry):
    tq = qT_ref.shape[2]
    nchunks = vT_ref.shape[1]
    zeros = jnp.zeros((HEAD_DIM, tq), BF16)
    for h in range(N_Q_HEADS):
        qh = qT_ref[0, h * HEAD_DIM:(h + 1) * HEAD_DIM, :]
        parts = [zeros] * N_KV_HEADS
        parts[h // Q_PER_KV] = qh
        qext[h] = jnp.concatenate(parts, axis=0)

    def with_ones(vT_all, g):
        ones = jnp.ones((BF16_ROWS, vT_all.shape[1]), BF16)
        return jnp.concatenate([vT_all[g * HEAD_DIM:(g + 1) * HEAD_DIM], ones], axis=0)

    def scores(c, h):
        kc = k_ref[0, pl.ds(pl.multiple_of(c * KV_CHUNK, KV_CHUNK), KV_CHUNK), :]
        return jnp.dot(kc, qext[h], preferred_element_type=F32)

    for n in range(QK_AHEAD):
        first = scores(n // N_Q_HEADS, n % N_Q_HEADS)
        s_carry[n] = first
        smax_carry[n] = jnp.max(first, axis=0, keepdims=True)

    km = km_ref[...]
    vTm = [with_ones(vTm_ref[...], g) for g in range(N_KV_HEADS)]
    s_meta = [jnp.dot(km, qext[h], preferred_element_type=F32) for h in range(N_Q_HEADS)]
    p_meta = []
    for h in range(N_Q_HEADS):
        m = jnp.max(s_meta[h], axis=0, keepdims=True)
        m_s[0, h] = m
        p_meta.append(jnp.exp2(s_meta[h] - m).astype(BF16))
    for h in range(N_Q_HEADS):
        acc_s[0, h] = jnp.dot(vTm[h // Q_PER_KV], p_meta[h], preferred_element_type=F32)[:ACC_ROWS]

    def values(c):
        return [with_ones(vT_ref[0, c], g) for g in range(N_KV_HEADS)]

    def exact_trip(i, src, dst):
        def chunk(j, carry):
            c = i * CHUNKS_PER_STEP + j
            rd = jnp.where(j == 0, src, dst)
            vc = values(c)
            for h in range(N_Q_HEADS):
                s = scores(c, h)
                m_old = m_s[rd, h]
                m_new = jnp.maximum(m_old, jnp.max(s, axis=0, keepdims=True))
                p = jnp.exp2(s - m_new)
                pv = jnp.dot(vc[h // Q_PER_KV], p.astype(BF16), preferred_element_type=F32)
                acc_s[dst, h] = jnp.exp2(m_old - m_new) * acc_s[rd, h] + pv[:ACC_ROWS]
                m_s[dst, h] = m_new
            return carry
        lax.fori_loop(0, CHUNKS_PER_STEP, chunk, 0)

    def fast_trip(i, src, dst):
        vcs = [values(i * CHUNKS_PER_STEP + j) for j in range(CHUNKS_PER_STEP)]
        units = [(j, h) for j in range(CHUNKS_PER_STEP) for h in range(N_Q_HEADS)]
        m_start = [m_s[src, h] for h in range(N_Q_HEADS)]
        pending = [(s_carry[n], smax_carry[n]) for n in range(QK_AHEAD)]
        excess = jnp.full((1, tq), -jnp.inf, F32)
        for n in range(QK_AHEAD):
            excess = jnp.maximum(excess, pending[n][1] - m_start[units[n][1]])
        for n, (j, h) in enumerate(units):
            rd = src if j == 0 else dst
            s, s_max = pending.pop(0)
            if n + QK_AHEAD < len(units):
                ja, ha = units[n + QK_AHEAD]
                ahead = scores(i * CHUNKS_PER_STEP + ja, ha)
                ahead_max = jnp.max(ahead, axis=0, keepdims=True)
                excess = jnp.maximum(excess, ahead_max - m_start[ha])
                pending.append((ahead, ahead_max))
            else:
                nxt = n + QK_AHEAD - len(units)
                c_next = jnp.minimum((i + 1) * CHUNKS_PER_STEP + units[nxt][0], nchunks - 1)
                ahead = scores(c_next, units[nxt][1])
                s_carry[nxt] = ahead
                smax_carry[nxt] = jnp.max(ahead, axis=0, keepdims=True)
            m_ref = m_s[rd, h]
            p = jnp.exp2(s - m_ref)
            pv = jnp.dot(vcs[j][h // Q_PER_KV], p.astype(BF16), preferred_element_type=F32)
            m_new = jnp.maximum(m_ref, s_max)
            acc_s[dst, h] = (acc_s[rd, h] + pv[:ACC_ROWS]) * jnp.exp2(m_ref - m_new)
            m_s[dst, h] = m_new

        @pl.when(jnp.logical_not(jnp.max(excess) <= SAFE_EXCESS))
        def _():
            exact_trip(i, src, dst)

    def kv_step(i2, carry):
        fast_trip(2 * i2, 0, 1)
        fast_trip(2 * i2 + 1, 1, 0)
        return carry

    assert nchunks % (2 * CHUNKS_PER_STEP) == 0
    lax.fori_loop(0, nchunks // (2 * CHUNKS_PER_STEP), kv_step, 0)

    oT = jnp.concatenate([acc_s[0, h, :HEAD_DIM] / acc_s[0, h, HEAD_DIM:HEAD_DIM + 1]
                          for h in range(N_Q_HEADS)], axis=0)
    ms = jnp.mean(oT * oT, axis=0, keepdims=True)
    oT = oT * lax.rsqrt(ms + EPS) * og_ref[...]
    o_ref[0] = oT.T.astype(BF16)


def _attention(qT, k, vT, k_meta, vT_meta, og_col, tq):
    B, _, S = qT.shape
    nchunks = S // KV_CHUNK
    return pl.pallas_call(
        _attn_kernel,
        grid=(B, S // tq),
        in_specs=[
            pl.BlockSpec((1, ATTN_WIDTH, tq), lambda b, i: (b, 0, i)),
            pl.BlockSpec((1, S, KV_WIDTH), lambda b, i: (b, 0, 0)),
            pl.BlockSpec((1, nchunks, KV_WIDTH, KV_CHUNK), lambda b, i: (b, 0, 0, 0)),
            pl.BlockSpec((N_META, KV_WIDTH), lambda b, i: (0, 0)),
            pl.BlockSpec((KV_WIDTH, N_META), lambda b, i: (0, 0)),
            pl.BlockSpec((ATTN_WIDTH, 1), lambda b, i: (0, 0)),
        ],
        out_specs=pl.BlockSpec((1, tq, ATTN_WIDTH), lambda b, i: (b, i, 0)),
        out_shape=jax.ShapeDtypeStruct((B, S, ATTN_WIDTH), BF16),
        scratch_shapes=[
            pltpu.VMEM((N_Q_HEADS, KV_WIDTH, tq), BF16),
            pltpu.VMEM((2, N_Q_HEADS, 1, tq), F32),
            pltpu.VMEM((2, N_Q_HEADS, ACC_ROWS, tq), F32),
            pltpu.VMEM((QK_AHEAD, KV_CHUNK, tq), F32),
            pltpu.VMEM((QK_AHEAD, 1, tq), F32),
        ],
        compiler_params=pltpu.CompilerParams(
            dimension_semantics=("parallel", "parallel"), vmem_limit_bytes=VMEM_LIMIT_BYTES),
        name="attention",
    )(qT, k, vT, k_meta, vT_meta, og_col)


def _post_kernel(x_ref, ao_ref, ly_ref, lgate_ref, lg_ref, woa_ref, wol_ref, fg_ref, wg_ref, wu_ref, wd_ref,
                 og_ref, o_ref, *, n_ffn_chunks):
    lo = _rms_rows(ly_ref[0] * _gelu_tanh(lgate_ref[0]), lg_ref[...]).astype(BF16)
    h = (x_ref[0]
         + jnp.dot(ao_ref[0], woa_ref[...], preferred_element_type=F32)
         + jnp.dot(lo, wol_ref[...], preferred_element_type=F32))
    xn = _rms_rows(h, fg_ref[...]).astype(BF16)
    fw = wg_ref.shape[1] // n_ffn_chunks
    for c in range(n_ffn_chunks):
        gate = jnp.dot(xn, wg_ref[:, c * fw:(c + 1) * fw], preferred_element_type=F32)
        up = jnp.dot(xn, wu_ref[:, c * fw:(c + 1) * fw], preferred_element_type=F32)
        act = (gate * jax.nn.sigmoid(gate) * up).astype(BF16)
        h = h + jnp.dot(act, wd_ref[c * fw:(c + 1) * fw, :], preferred_element_type=F32)
    o_ref[0] = _rms_rows(h, og_ref[...])


def _post(x, attn_o, lru_y, lru_gate, lru_g, wo_a, wo_l, ffn_g, w_gate, w_up, w_down, out_g, tm):
    B, S, D = x.shape
    F = w_gate.shape[1]
    n_ffn_chunks = 1
    const = lambda b, i: (0, 0)
    resident = functools.partial(pl.BlockSpec, index_map=const, pipeline_mode=pl.Buffered(1))
    row = lambda w: pl.BlockSpec((1, tm, w), lambda b, i: (b, i, 0))
    return pl.pallas_call(
        functools.partial(_post_kernel, n_ffn_chunks=n_ffn_chunks),
        grid=(B, S // tm),
        in_specs=[
            row(D), row(ATTN_WIDTH), row(lru_y.shape[2]), row(lru_y.shape[2]),
            resident((1, lru_y.shape[2])),
            resident(wo_a.shape), resident(wo_l.shape),
            resident((1, D)),
            resident(w_gate.shape), resident(w_up.shape), resident(w_down.shape),
            resident((1, D)),
        ],
        out_specs=row(D),
        out_shape=jax.ShapeDtypeStruct((B, S, D), F32),
        compiler_params=pltpu.CompilerParams(
            dimension_semantics=("parallel", "parallel"), vmem_limit_bytes=VMEM_LIMIT_BYTES),
        name="post",
    )(x, attn_o, lru_y, lru_gate, lru_g, wo_a, wo_l, ffn_g, w_gate, w_up, w_down, out_g)


def _rope_tables(n_tokens):
    n_rows = n_tokens // GRID_W
    row = jnp.repeat(jnp.arange(n_rows), GRID_W).astype(F32)
    col = jnp.tile(jnp.arange(GRID_W), n_rows).astype(F32)
    freqs = ROPE_THETA ** (-jnp.arange(0, ROPE_AXIS_DIM, 2, dtype=F32) / ROPE_AXIS_DIM)
    ar = freqs[:, None] * row[None, :]
    ac = freqs[:, None] * col[None, :]
    cos = jnp.concatenate([jnp.cos(ar), jnp.cos(ar), jnp.cos(ac), jnp.cos(ac)], axis=0)
    sin = jnp.concatenate([-jnp.sin(ar), jnp.sin(ar), -jnp.sin(ac), jnp.sin(ac)], axis=0)
    return cos, sin


def _block_diag_pairs(w):
    nb, bw, _ = w.shape
    w = w.reshape(nb // 2, 2, bw, bw)
    z = jnp.zeros((nb // 2, bw, bw), w.dtype)
    top = jnp.concatenate([w[:, 0], z], axis=2)
    bot = jnp.concatenate([z, w[:, 1]], axis=2)
    return jnp.concatenate([top, bot], axis=1)


def _row_tile(S):
    return 512 if S % 512 == 0 else S


def _trunk(x, prep):
    B, S, D = x.shape
    tm = _row_tile(S)
    cosT, sinT = _rope_tables(S)
    qT, k, vT, lin, lgate = _inproj(x, prep["mix_g"], prep["wqkvT"], prep["wlru"], prep["qg"], prep["kg"],
                                    cosT, sinT, 2 * tm if S % (2 * tm) == 0 else tm)
    lru_y = _lru(prep["lin_meta"], lin, prep["conv_w"], prep["conv_b"], prep["w_gates"],
                 prep["b_gates"], prep["lam"])
    attn_o = _attention(qT, k, vT, prep["k_meta"], prep["vT_meta"], prep["attn_g"], tm)
    return _post(x, attn_o, lru_y, lgate, prep["lru_g"], prep["wo_a"], prep["wo_l"], prep["ffn_g"],
                 prep["w_gate"], prep["w_up"], prep["w_down"], prep["final_g"], tm)


def kernel(x_prompt, x_sample, meta_tokens, norm_mix_g, w_in, q_norm_g, k_norm_g, conv_w, conv_b, lru_w_a, lru_b_a, lru_w_x, lru_b_x, lru_lam, attn_out_g, lru_out_g, w_out, norm_ffn_g, w_gate_up, w_down, final_norm_g):
    assert w_in.shape[0] == 1, "single-layer trunk"
    D = x_prompt.shape[-1]
    lw = D - ATTN_WIDTH
    qkv_w = ATTN_WIDTH + 2 * KV_WIDTH
    w = w_in[0]
    prep = {
        "mix_g": norm_mix_g[0][None],
        "wqkvT": w[:, :qkv_w].T.astype(BF16),
        "wlru": w[:, qkv_w:].astype(BF16),
        "qg": q_norm_g[0][:, None],
        "kg": k_norm_g[0][:, None],
        "conv_w": conv_w[0],
        "conv_b": conv_b[0][None],
        "lam": lru_lam[0],
        "attn_g": attn_out_g[0][:, None],
        "lru_g": lru_out_g[0][None],
        "wo_a": w_out[0][:ATTN_WIDTH].astype(BF16),
        "wo_l": w_out[0][ATTN_WIDTH:].astype(BF16),
        "ffn_g": norm_ffn_g[0][None],
        "w_gate": w_gate_up[0][:, :w_gate_up.shape[2] // 2].astype(BF16),
        "w_up": w_gate_up[0][:, w_gate_up.shape[2] // 2:].astype(BF16),
        "w_down": w_down[0].astype(BF16),
        "final_g": final_norm_g[None],
    }
    wa = [_block_diag_pairs(lru_w_a[0, d]) for d in range(2)]
    wx = [_block_diag_pairs(lru_w_x[0, d]) for d in range(2)]
    prep["w_gates"] = (0.5 * jnp.concatenate([wa[0], wx[0], wa[1], wx[1]], axis=2)).astype(BF16)
    ncg = lw // LRU_LANES
    bias = [b.reshape(ncg, 1, LRU_LANES) for b in (lru_b_a[0, 0], lru_b_x[0, 0], lru_b_a[0, 1], lru_b_x[0, 1])]
    prep["b_gates"] = 0.5 * jnp.concatenate(bias, axis=2)

    meta = jnp.zeros((1, META_PAD, D), F32).at[0, :N_META].set(meta_tokens)
    ones = jnp.ones((HEAD_DIM, META_PAD), F32)
    _, k_m, vT_m, lin_m, _ = _inproj(meta, prep["mix_g"], prep["wqkvT"], prep["wlru"], prep["qg"], prep["kg"],
                                     ones, jnp.zeros_like(ones), META_PAD)
    prep["k_meta"] = k_m[0, :N_META]
    prep["vT_meta"] = vT_m[0, 0, :, :N_META]
    prep["lin_meta"] = lin_m[0, :N_META]
    return _trunk(x_prompt, prep), _trunk(x_sample, prep)
```

```python
import functools

import jax
import jax.numpy as jnp
import numpy as np
from jax import lax
from jax.experimental import pallas as pl
from jax.experimental.pallas import tpu as pltpu

F32 = jnp.float32
BF16 = jnp.bfloat16

N_META = 16
GRID_W = 64
HEAD_DIM = 64
N_Q_HEADS = 8
N_KV_HEADS = 2
Q_PER_KV = N_Q_HEADS // N_KV_HEADS
ATTN_WIDTH = N_Q_HEADS * HEAD_DIM
KV_WIDTH = N_KV_HEADS * HEAD_DIM
LRU_BLOCKS = 8
CONV_W = 4
CONV_LEFT = 2
LRU_C = 8.0
ROPE_AXIS_DIM = HEAD_DIM // 2
ROPE_THETA = 10000.0
EPS = 1e-6
LOG2_E = float(np.log2(np.e))

LANES = 128
SUBLANES = 8
BF16_ROWS = 16
VMEM_LIMIT_BYTES = 56 * 1024 * 1024

LRU_LANES = LANES
SCAN_SEGMENTS = SUBLANES
SCAN_GROUPS = 2
SCAN_UNROLL_MAX = 3
CONV_HALO = SUBLANES
KV_CHUNK = 256
CHUNKS_PER_STEP = 4
QK_AHEAD = 2
SAFE_EXCESS = 64.0
ACC_ROWS = HEAD_DIM + SUBLANES
META_PAD = KV_CHUNK


def _rms_rows(x, g):
    ms = jnp.mean(x * x, axis=-1, keepdims=True)
    return x * lax.rsqrt(ms + EPS) * g


def _head_norm_rope(xT, g_col, cos, sin_signed):
    nh = xT.shape[0] // HEAD_DIM
    x = xT.reshape(nh, HEAD_DIM, xT.shape[1])
    ms = jnp.mean(x * x, axis=1, keepdims=True)
    x = x * lax.rsqrt(ms + EPS) * g_col[None]
    q = ROPE_AXIS_DIM // 2
    partner = jnp.concatenate(
        [x[:, q:2 * q], x[:, 0:q], x[:, 3 * q:4 * q], x[:, 2 * q:3 * q]], axis=1)
    x = x * cos[None] + partner * sin_signed[None]
    return x.reshape(xT.shape)


def _inproj_kernel(x_ref, g_ref, wqkv_ref, wlru_ref, qg_ref, kg_ref, cos_ref, sin_ref,
                   qT_ref, k_ref, vT_ref, lin_ref, lgate_ref):
    w = lin_ref.shape[2]

    def project(j):
        rows = slice(j * KV_CHUNK, (j + 1) * KV_CHUNK)
        xn = _rms_rows(x_ref[0, rows, :], g_ref[...]).astype(BF16)
        qkvT = lax.dot_general(wqkv_ref[...], xn, (((1,), (1,)), ((), ())),
                               preferred_element_type=F32)
        lru = jnp.dot(xn, wlru_ref[...], preferred_element_type=F32)
        lin_ref[0, rows, :] = lru[:, :w]
        lgate_ref[0, rows, :] = lru[:, w:]
        return qkvT

    def finish(j, qkvT):
        rows = slice(j * KV_CHUNK, (j + 1) * KV_CHUNK)
        cos = cos_ref[:, rows]
        sin = sin_ref[:, rows]
        qT = _head_norm_rope(qkvT[:ATTN_WIDTH], qg_ref[...], cos, sin) * (HEAD_DIM ** -0.5 * LOG2_E)
        qT_ref[0, :, rows] = qT.astype(BF16)
        kT = _head_norm_rope(qkvT[ATTN_WIDTH:ATTN_WIDTH + KV_WIDTH], kg_ref[...], cos, sin)
        k_ref[0, rows, :] = kT.T.astype(BF16)
        vT_ref[0, j] = qkvT[ATTN_WIDTH + KV_WIDTH:].astype(BF16)

    pieces = vT_ref.shape[1]
    pending = project(0)
    for j in range(pieces):
        current = pending
        if j + 1 < pieces:
            pending = project(j + 1)
        finish(j, current)


def _inproj(x, g, wqkvT, wlru, qg_col, kg_col, cosT, sinT, tm):
    B, S, D = x.shape
    lw = wlru.shape[1] // 2
    nt = S // tm
    const = lambda b, i: (0, 0)
    return pl.pallas_call(
        _inproj_kernel,
        grid=(B, nt),
        in_specs=[
            pl.BlockSpec((1, tm, D), lambda b, i: (b, i, 0)),
            pl.BlockSpec((1, D), const),
            pl.BlockSpec(wqkvT.shape, const),
            pl.BlockSpec(wlru.shape, const),
            pl.BlockSpec((HEAD_DIM, 1), const),
            pl.BlockSpec((HEAD_DIM, 1), const),
            pl.BlockSpec((HEAD_DIM, tm), lambda b, i: (0, i)),
            pl.BlockSpec((HEAD_DIM, tm), lambda b, i: (0, i)),
        ],
        out_specs=[
            pl.BlockSpec((1, ATTN_WIDTH, tm), lambda b, i: (b, 0, i)),
            pl.BlockSpec((1, tm, KV_WIDTH), lambda b, i: (b, i, 0)),
            pl.BlockSpec((1, tm // KV_CHUNK, KV_WIDTH, KV_CHUNK), lambda b, i: (b, i, 0, 0)),
            pl.BlockSpec((1, tm, lw), lambda b, i: (b, i, 0)),
            pl.BlockSpec((1, tm, lw), lambda b, i: (b, i, 0)),
        ],
        out_shape=[
            jax.ShapeDtypeStruct((B, ATTN_WIDTH, S), BF16),
            jax.ShapeDtypeStruct((B, S, KV_WIDTH), BF16),
            jax.ShapeDtypeStruct((B, S // KV_CHUNK, KV_WIDTH, KV_CHUNK), BF16),
            jax.ShapeDtypeStruct((B, S, lw), F32),
            jax.ShapeDtypeStruct((B, S, lw), F32),
        ],
        compiler_params=pltpu.CompilerParams(
            dimension_semantics=("parallel", "parallel"), vmem_limit_bytes=VMEM_LIMIT_BYTES),
        name="inproj",
    )(x, g, wqkvT, wlru, qg_col, kg_col, cosT, sinT)


def _gelu_tanh(x):
    return 0.5 * x * (1.0 + jnp.tanh(np.sqrt(2.0 / np.pi).astype(np.float32) * (x + 0.044715 * (x * x * x))))


def _lru_kernel(xm_ref, x_ref, cw_ref, cb_ref, wg_ref, bg_ref, lam_ref, y_ref,
                X, Af, Uf, Ab, Ub, *, L, T, TY, K):
    S = L - N_META
    H = CONV_HALO
    seg = L // (SCAN_SEGMENTS * SCAN_GROUPS)
    zeros = jnp.zeros((H, LRU_LANES), F32)
    X[0:H] = zeros
    X[H + L:H + L + H] = zeros
    X[H:H + N_META] = xm_ref[...]
    X[H + N_META:H + L] = x_ref[0]

    lam = lam_ref[...]
    z = -lam
    softplus = jnp.maximum(z, 0.0) + jnp.log1p(jnp.exp(-jnp.abs(z)))
    half_decay = (0.5 * LRU_C) * softplus
    cw = cw_ref[...]
    cb = cb_ref[...]
    bg = bg_ref[...]

    def gate_chunk(i, carry):
        r0 = pl.multiple_of(i * T, SUBLANES)
        c = cb
        for j in range(CONV_W):
            c = c + X[pl.ds(r0 + (H - CONV_LEFT + j), T), :] * cw[j:j + 1]
        gh = jnp.dot(c.astype(BF16), wg_ref[...], preferred_element_type=F32) + bg
        for d, (A, U) in enumerate(((Af, Uf), (Ab, Ub))):
            base = 2 * d * LRU_LANES
            t_r = jnp.tanh(gh[:, base:base + LRU_LANES])
            t_i = jnp.tanh(gh[:, base + LRU_LANES:base + 2 * LRU_LANES])
            hd = half_decay[d:d + 1]
            nla = hd * t_r + hd
            a = jnp.exp2(nla * (-LOG2_E))
            x = jnp.tanh(nla) * (1.0 + a * a)
            mult = jnp.where(x == 0.0, 0.0, x * lax.rsqrt(x))
            A[pl.ds(r0, T), :] = a
            U[pl.ds(r0, T), :] = (mult * c) * (0.5 * t_i + 0.5)
        return carry

    lax.fori_loop(0, L // T, gate_chunk, 0)

    G = SCAN_GROUPS

    def seg_rows(g, j):
        return pl.ds(g * SCAN_SEGMENTS * seg + j, SCAN_SEGMENTS, stride=seg)

    def scan_block(jb, carry):
        state = [list(c) for c in carry]
        j0 = jb * K
        fwd = [[(Af[seg_rows(g, j0 + k), :], Uf[seg_rows(g, j0 + k), :]) for k in range(K)] for g in range(G)]
        bwd = [[(Ab[seg_rows(g, seg - 1 - j0 - k), :], Ub[seg_rows(g, seg - 1 - j0 - k), :])
                for k in range(K)] for g in range(G)]
        out = []
        for k in range(K):
            for g in range(G):
                hf, pf, hb, pb = state[g]
                a, u = fwd[g][k]
                hf = a * hf + u
                pf = a * pf
                a, u = bwd[g][k]
                hb = a * hb + u
                pb = a * pb
                state[g] = [hf, pf, hb, pb]
                out.append((g, k, hf, pf, hb, pb))
        for g, k, hf, pf, hb, pb in out:
            Uf[seg_rows(g, j0 + k), :] = hf
            Af[seg_rows(g, j0 + k), :] = pf
            Ub[seg_rows(g, seg - 1 - j0 - k), :] = hb
            Ab[seg_rows(g, seg - 1 - j0 - k), :] = pb
        return tuple(tuple(c) for c in state)

    zero = jnp.zeros((SCAN_SEGMENTS, LRU_LANES), F32)
    one = jnp.ones((SCAN_SEGMENTS, LRU_LANES), F32)
    ends = lax.fori_loop(0, seg // K, scan_block, tuple((zero, one, zero, one) for _ in range(G)))

    order = [(g, r) for g in range(G) for r in range(SCAN_SEGMENTS)]
    row = jnp.zeros((1, LRU_LANES), F32)
    rows = {}
    for g, r in order:
        rows[g, r] = row
        row = ends[g][0][r:r + 1] + ends[g][1][r:r + 1] * row
    cin_f = [jnp.concatenate([rows[g, r] for r in range(SCAN_SEGMENTS)], axis=0) for g in range(G)]
    row = jnp.zeros((1, LRU_LANES), F32)
    rows = {}
    for g, r in reversed(order):
        rows[g, r] = row
        row = ends[g][2][r:r + 1] + ends[g][3][r:r + 1] * row
    cin_b = [jnp.concatenate([rows[g, r] for r in range(SCAN_SEGMENTS)], axis=0) for g in range(G)]

    def fix_block(jb, carry):
        j0 = jb * K
        h = [(g, k, Uf[seg_rows(g, j0 + k), :] + Af[seg_rows(g, j0 + k), :] * cin_f[g]
              + Ub[seg_rows(g, j0 + k), :] + Ab[seg_rows(g, j0 + k), :] * cin_b[g])
             for k in range(K) for g in range(G)]
        for g, k, v in h:
            Uf[seg_rows(g, j0 + k), :] = v
        return carry

    lax.fori_loop(0, seg // K, fix_block, 0)

    def out_chunk(i, carry):
        r0 = pl.multiple_of(i * TY, SUBLANES)
        y_ref[0, pl.ds(r0, TY), :] = Uf[pl.ds(N_META + r0, TY), :]
        return carry

    lax.fori_loop(0, S // TY, out_chunk, 0)


def _pick_chunk(n, align, cap):
    best = align
    for t in range(align, cap + 1, align):
        if n % t == 0:
            best = t
    return best


def _lru(xin_meta, xin, cw, cb, wg, bg, lam):
    B, S, W = xin.shape
    L = S + N_META
    assert L % (SCAN_SEGMENTS * SCAN_GROUPS) == 0 and W % LRU_LANES == 0
    ncg = W // LRU_LANES
    T = _pick_chunk(L, BF16_ROWS, 512)
    TY = _pick_chunk(S, SUBLANES, 512)
    assert L % T == 0 and S % TY == 0
    scratch_rows = L + 2 * CONV_HALO
    K = _pick_chunk(L // (SCAN_SEGMENTS * SCAN_GROUPS), 1, SCAN_UNROLL_MAX)
    kern = functools.partial(_lru_kernel, L=L, T=T, TY=TY, K=K)
    return pl.pallas_call(
        kern,
        grid=(B, ncg),
        in_specs=[
            pl.BlockSpec((N_META, LRU_LANES), lambda b, c: (0, c)),
            pl.BlockSpec((1, S, LRU_LANES), lambda b, c: (b, 0, c)),
            pl.BlockSpec((CONV_W, LRU_LANES), lambda b, c: (0, c)),
            pl.BlockSpec((1, LRU_LANES), lambda b, c: (0, c)),
            pl.BlockSpec((None, LRU_LANES, 4 * LRU_LANES), lambda b, c: (c, 0, 0)),
            pl.BlockSpec((None, 1, 4 * LRU_LANES), lambda b, c: (c, 0, 0)),
            pl.BlockSpec((2, LRU_LANES), lambda b, c: (0, c)),
        ],
        out_specs=pl.BlockSpec((1, S, LRU_LANES), lambda b, c: (b, 0, c)),
        out_shape=jax.ShapeDtypeStruct((B, S, W), F32),
        scratch_shapes=[pltpu.VMEM((scratch_rows, LRU_LANES), F32)]
        + [pltpu.VMEM((L, LRU_LANES), F32) for _ in range(4)],
        compiler_params=pltpu.CompilerParams(
            dimension_semantics=("parallel", "parallel"), vmem_limit_bytes=VMEM_LIMIT_BYTES),
        name="lru",
    )(xin_meta, xin, cw, cb, wg, bg, lam)


def _attn_kernel(qT_ref, k_ref, vT_ref, km_ref, vTm_ref, og_ref, o_ref,
                 qext, m_s, acc_s, s_carry, smax_carry):
    tq = qT_ref.shape[2]
    nchunks = vT_ref.shape[1]
    zeros = jnp.zeros((HEAD_DIM, tq), BF16)
    for h in range(N_Q_HEADS):
        qh = qT_ref[0, h * HEAD_DIM:(h + 1) * HEAD_DIM, :]
        parts = [zeros] * N_KV_HEADS
        parts[h // Q_PER_KV] = qh
        qext[h] = jnp.concatenate(parts, axis=0)

    def with_ones(vT_all, g):
        ones = jnp.ones((BF16_ROWS, vT_all.shape[1]), BF16)
        return jnp.concatenate([vT_all[g * HEAD_DIM:(g + 1) * HEAD_DIM], ones], axis=0)

    def scores(c, h):
        kc = k_ref[0, pl.ds(pl.multiple_of(c * KV_CHUNK, KV_CHUNK), KV_CHUNK), :]
        return jnp.dot(kc, qext[h], preferred_element_type=F32)

    def init_from_meta():
        km = km_ref[...]
        vTm = [with_ones(vTm_ref[...], g) for g in range(N_KV_HEADS)]
        s_meta = [jnp.dot(km, qext[h], preferred_element_type=F32) for h in range(N_Q_HEADS)]
        p_meta = []
        for h in range(N_Q_HEADS):
            m = jnp.max(s_meta[h], axis=0, keepdims=True)
            m_s[h] = m
            p_meta.append(jnp.exp2(s_meta[h] - m).astype(BF16))
        for h in range(N_Q_HEADS):
            acc_s[h] = jnp.dot(vTm[h // Q_PER_KV], p_meta[h], preferred_element_type=F32)[:ACC_ROWS]

    for n in range(QK_AHEAD):
        first = scores(n // N_Q_HEADS, n % N_Q_HEADS)
        s_carry[n] = first
        smax_carry[n] = jnp.max(first, axis=0, keepdims=True)
    init_from_meta()

    def values(c):
        return [with_ones(vT_ref[0, c], g) for g in range(N_KV_HEADS)]

    def exact_chunk(c, carry):
        vc = values(c)
        for h in range(N_Q_HEADS):
            s = scores(c, h)
            m_old = m_s[h]
            m_new = jnp.maximum(m_old, jnp.max(s, axis=0, keepdims=True))
            p = jnp.exp2(s - m_new)
            pv = jnp.dot(vc[h // Q_PER_KV], p.astype(BF16), preferred_element_type=F32)
            acc_s[h] = jnp.exp2(m_old - m_new) * acc_s[h] + pv[:ACC_ROWS]
            m_s[h] = m_new
        return carry

    def fast_trip(i, excess):
        vcs = [values(i * CHUNKS_PER_STEP + j) for j in range(CHUNKS_PER_STEP)]
        units = [(j, h) for j in range(CHUNKS_PER_STEP) for h in range(N_Q_HEADS)]
        m_start = [m_s[h] for h in range(N_Q_HEADS)]
        pending = [(s_carry[n], smax_carry[n]) for n in range(QK_AHEAD)]
        for n in range(QK_AHEAD):
            excess = jnp.maximum(excess, pending[n][1] - m_start[units[n][1]])
        for n, (j, h) in enumerate(units):
            s, s_max = pending.pop(0)
            if n + QK_AHEAD < len(units):
                ja, ha = units[n + QK_AHEAD]
                ahead = scores(i * CHUNKS_PER_STEP + ja, ha)
                ahead_max = jnp.max(ahead, axis=0, keepdims=True)
                excess = jnp.maximum(excess, ahead_max - m_start[ha])
                pending.append((ahead, ahead_max))
            else:
                nxt = n + QK_AHEAD - len(units)
                c_next = jnp.minimum((i + 1) * CHUNKS_PER_STEP + units[nxt][0], nchunks - 1)
                ahead = scores(c_next, units[nxt][1])
                s_carry[nxt] = ahead
                smax_carry[nxt] = jnp.max(ahead, axis=0, keepdims=True)
            m_ref = m_s[h]
            p = jnp.exp2(s - m_ref)
            pv = jnp.dot(vcs[j][h // Q_PER_KV], p.astype(BF16), preferred_element_type=F32)
            m_new = jnp.maximum(m_ref, s_max)
            acc_s[h] = (acc_s[h] + pv[:ACC_ROWS]) * jnp.exp2(m_ref - m_new)
            m_s[h] = m_new
        return excess

    assert nchunks % CHUNKS_PER_STEP == 0
    excess = lax.fori_loop(0, nchunks // CHUNKS_PER_STEP, fast_trip, jnp.full((1, tq), -jnp.inf, F32),
                           unroll=2)

    @pl.when(jnp.logical_not(jnp.max(excess) <= SAFE_EXCESS))
    def _():
        init_from_meta()
        lax.fori_loop(0, nchunks, exact_chunk, 0)

    oT = jnp.concatenate([acc_s[h, :HEAD_DIM] / acc_s[h, HEAD_DIM:HEAD_DIM + 1]
                          for h in range(N_Q_HEADS)], axis=0)
    ms = jnp.mean(oT * oT, axis=0, keepdims=True)
    oT = oT * lax.rsqrt(ms + EPS) * og_ref[...]
    o_ref[0] = oT.T.astype(BF16)


def _attention(qT, k, vT, k_meta, vT_meta, og_col, tq):
    B, _, S = qT.shape
    nchunks = S // KV_CHUNK
    return pl.pallas_call(
        _attn_kernel,
        grid=(B, S // tq),
        in_specs=[
            pl.BlockSpec((1, ATTN_WIDTH, tq), lambda b, i: (b, 0, i)),
            pl.BlockSpec((1, S, KV_WIDTH), lambda b, i: (b, 0, 0)),
            pl.BlockSpec((1, nchunks, KV_WIDTH, KV_CHUNK), lambda b, i: (b, 0, 0, 0)),
            pl.BlockSpec((N_META, KV_WIDTH), lambda b, i: (0, 0)),
            pl.BlockSpec((KV_WIDTH, N_META), lambda b, i: (0, 0)),
            pl.BlockSpec((ATTN_WIDTH, 1), lambda b, i: (0, 0)),
        ],
        out_specs=pl.BlockSpec((1, tq, ATTN_WIDTH), lambda b, i: (b, i, 0)),
        out_shape=jax.ShapeDtypeStruct((B, S, ATTN_WIDTH), BF16),
        scratch_shapes=[
            pltpu.VMEM((N_Q_HEADS, KV_WIDTH, tq), BF16),
            pltpu.VMEM((N_Q_HEADS, 1, tq), F32),
            pltpu.VMEM((N_Q_HEADS, ACC_ROWS, tq), F32),
            pltpu.VMEM((QK_AHEAD, KV_CHUNK, tq), F32),
            pltpu.VMEM((QK_AHEAD, 1, tq), F32),
        ],
        compiler_params=pltpu.CompilerParams(
            dimension_semantics=("parallel", "parallel"), vmem_limit_bytes=VMEM_LIMIT_BYTES),
        name="attention",
    )(qT, k, vT, k_meta, vT_meta, og_col)


def _post_kernel(x_ref, ao_ref, ly_ref, lgate_ref, lg_ref, woa_ref, wol_ref, fg_ref, wg_ref, wu_ref, wd_ref,
                 og_ref, o_ref, *, n_ffn_chunks):
    lo = _rms_rows(ly_ref[0] * _gelu_tanh(lgate_ref[0]), lg_ref[...]).astype(BF16)
    h = (x_ref[0]
         + jnp.dot(ao_ref[0], woa_ref[...], preferred_element_type=F32)
         + jnp.dot(lo, wol_ref[...], preferred_element_type=F32))
    xn = _rms_rows(h, fg_ref[...]).astype(BF16)
    fw = wg_ref.shape[1] // n_ffn_chunks
    for c in range(n_ffn_chunks):
        gate = jnp.dot(xn, wg_ref[:, c * fw:(c + 1) * fw], preferred_element_type=F32)
        up = jnp.dot(xn, wu_ref[:, c * fw:(c + 1) * fw], preferred_element_type=F32)
        act = (gate * jax.nn.sigmoid(gate) * up).astype(BF16)
        h = h + jnp.dot(act, wd_ref[c * fw:(c + 1) * fw, :], preferred_element_type=F32)
    o_ref[0] = _rms_rows(h, og_ref[...])


def _post(x, attn_o, lru_y, lru_gate, lru_g, wo_a, wo_l, ffn_g, w_gate, w_up, w_down, out_g, tm):
    B, S, D = x.shape
    F = w_gate.shape[1]
    n_ffn_chunks = 1
    const = lambda b, i: (0, 0)
    resident = functools.partial(pl.BlockSpec, index_map=const, pipeline_mode=pl.Buffered(1))
    row = lambda w: pl.BlockSpec((1, tm, w), lambda b, i: (b, i, 0))
    return pl.pallas_call(
        functools.partial(_post_kernel, n_ffn_chunks=n_ffn_chunks),
        grid=(B, S // tm),
        in_specs=[
            row(D), row(ATTN_WIDTH), row(lru_y.shape[2]), row(lru_y.shape[2]),
            resident((1, lru_y.shape[2])),
            resident(wo_a.shape), resident(wo_l.shape),
            resident((1, D)),
            resident(w_gate.shape), resident(w_up.shape), resident(w_down.shape),
            resident((1, D)),
        ],
        out_specs=row(D),
        out_shape=jax.ShapeDtypeStruct((B, S, D), F32),
        compiler_params=pltpu.CompilerParams(
            dimension_semantics=("parallel", "parallel"), vmem_limit_bytes=VMEM_LIMIT_BYTES),
        name="post",
    )(x, attn_o, lru_y, lru_gate, lru_g, wo_a, wo_l, ffn_g, w_gate, w_up, w_down, out_g)


def _rope_tables(n_tokens):
    n_rows = n_tokens // GRID_W
    row = jnp.repeat(jnp.arange(n_rows), GRID_W).astype(F32)
    col = jnp.tile(jnp.arange(GRID_W), n_rows).astype(F32)
    freqs = ROPE_THETA ** (-jnp.arange(0, ROPE_AXIS_DIM, 2, dtype=F32) / ROPE_AXIS_DIM)
    ar = freqs[:, None] * row[None, :]
    ac = freqs[:, None] * col[None, :]
    cos = jnp.concatenate([jnp.cos(ar), jnp.cos(ar), jnp.cos(ac), jnp.cos(ac)], axis=0)
    sin = jnp.concatenate([-jnp.sin(ar), jnp.sin(ar), -jnp.sin(ac), jnp.sin(ac)], axis=0)
    return cos, sin


def _block_diag_pairs(w):
    nb, bw, _ = w.shape
    w = w.reshape(nb // 2, 2, bw, bw)
    z = jnp.zeros((nb // 2, bw, bw), w.dtype)
    top = jnp.concatenate([w[:, 0], z], axis=2)
    bot = jnp.concatenate([z, w[:, 1]], axis=2)
    return jnp.concatenate([top, bot], axis=1)


def _row_tile(S):
    return 512 if S % 512 == 0 else S


def _trunk(x, prep):
    B, S, D = x.shape
    tm = _row_tile(S)
    cosT, sinT = _rope_tables(S)
    qT, k, vT, lin, lgate = _inproj(x, prep["mix_g"], prep["wqkvT"], prep["wlru"], prep["qg"], prep["kg"],
                                    cosT, sinT, 2 * tm if S % (2 * tm) == 0 else tm)
    lru_y = _lru(prep["lin_meta"], lin, prep["conv_w"], prep["conv_b"], prep["w_gates"],
                 prep["b_gates"], prep["lam"])
    attn_o = _attention(qT, k, vT, prep["k_meta"], prep["vT_meta"], prep["attn_g"], tm)
    return _post(x, attn_o, lru_y, lgate, prep["lru_g"], prep["wo_a"], prep["wo_l"], prep["ffn_g"],
                 prep["w_gate"], prep["w_up"], prep["w_down"], prep["final_g"], tm)


def kernel(x_prompt, x_sample, meta_tokens, norm_mix_g, w_in, q_norm_g, k_norm_g, conv_w, conv_b, lru_w_a, lru_b_a, lru_w_x, lru_b_x, lru_lam, attn_out_g, lru_out_g, w_out, norm_ffn_g, w_gate_up, w_down, final_norm_g):
    assert w_in.shape[0] == 1, "single-layer trunk"
    D = x_prompt.shape[-1]
    lw = D - ATTN_WIDTH
    qkv_w = ATTN_WIDTH + 2 * KV_WIDTH
    w = w_in[0]
    prep = {
        "mix_g": norm_mix_g[0][None],
        "wqkvT": w[:, :qkv_w].T.astype(BF16),
        "wlru": w[:, qkv_w:].astype(BF16),
        "qg": q_norm_g[0][:, None],
        "kg": k_norm_g[0][:, None],
        "conv_w": conv_w[0],
        "conv_b": conv_b[0][None],
        "lam": lru_lam[0],
        "attn_g": attn_out_g[0][:, None],
        "lru_g": lru_out_g[0][None],
        "wo_a": w_out[0][:ATTN_WIDTH].astype(BF16),
        "wo_l": w_out[0][ATTN_WIDTH:].astype(BF16),
        "ffn_g": norm_ffn_g[0][None],
        "w_gate": w_gate_up[0][:, :w_gate_up.shape[2] // 2].astype(BF16),
        "w_up": w_gate_up[0][:, w_gate_up.shape[2] // 2:].astype(BF16),
        "w_down": w_down[0].astype(BF16),
        "final_g": final_norm_g[None],
    }
    wa = [_block_diag_pairs(lru_w_a[0, d]) for d in range(2)]
    wx = [_block_diag_pairs(lru_w_x[0, d]) for d in range(2)]
    prep["w_gates"] = (0.5 * jnp.concatenate([wa[0], wx[0], wa[1], wx[1]], axis=2)).astype(BF16)
    ncg = lw // LRU_LANES
    bias = [b.reshape(ncg, 1, LRU_LANES) for b in (lru_b_a[0, 0], lru_b_x[0, 0], lru_b_a[0, 1], lru_b_x[0, 1])]
    prep["b_gates"] = 0.5 * jnp.concatenate(bias, axis=2)

    meta = jnp.zeros((1, META_PAD, D), F32).at[0, :N_META].set(meta_tokens)
    ones = jnp.ones((HEAD_DIM, META_PAD), F32)
    _, k_m, vT_m, lin_m, _ = _inproj(meta, prep["mix_g"], prep["wqkvT"], prep["wlru"], prep["qg"], prep["kg"],
                                     ones, jnp.zeros_like(ones), META_PAD)
    prep["k_meta"] = k_m[0, :N_META]
    prep["vT_meta"] = vT_m[0, 0, :, :N_META]
    prep["lin_meta"] = lin_m[0, :N_META]
    return _trunk(x_prompt, prep), _trunk(x_sample, prep)
```

```python
import functools

import jax
import jax.numpy as jnp
import numpy as np
from jax import lax
from jax.experimental import pallas as pl
from jax.experimental.pallas import tpu as pltpu

F32 = jnp.float32
BF16 = jnp.bfloat16

N_META = 16
GRID_W = 64
HEAD_DIM = 64
N_Q_HEADS = 8
N_KV_HEADS = 2
Q_PER_KV = N_Q_HEADS // N_KV_HEADS
ATTN_WIDTH = N_Q_HEADS * HEAD_DIM
KV_WIDTH = N_KV_HEADS * HEAD_DIM
LRU_BLOCKS = 8
CONV_W = 4
CONV_LEFT = 2
LRU_C = 8.0
ROPE_AXIS_DIM = HEAD_DIM // 2
ROPE_THETA = 10000.0
EPS = 1e-6
LOG2_E = float(np.log2(np.e))

LANES = 128
SUBLANES = 8
BF16_ROWS = 16
VMEM_LIMIT_BYTES = 56 * 1024 * 1024

LRU_LANES = LANES
SCAN_SEGMENTS = SUBLANES
SCAN_GROUPS = 2
SCAN_UNROLL_MAX = 3
CONV_HALO = SUBLANES
KV_CHUNK = 256
CHUNKS_PER_STEP = 4
QK_AHEAD = 2
SAFE_EXCESS = 64.0
ACC_ROWS = HEAD_DIM + SUBLANES
POST_PIECES = 2
META_PAD = KV_CHUNK


def _rms_rows(x, g):
    ms = jnp.mean(x * x, axis=-1, keepdims=True)
    return x * lax.rsqrt(ms + EPS) * g


def _head_norm_rope(xT, g_col, cos, sin_signed):
    nh = xT.shape[0] // HEAD_DIM
    x = xT.reshape(nh, HEAD_DIM, xT.shape[1])
    ms = jnp.mean(x * x, axis=1, keepdims=True)
    x = x * lax.rsqrt(ms + EPS) * g_col[None]
    q = ROPE_AXIS_DIM // 2
    partner = jnp.concatenate(
        [x[:, q:2 * q], x[:, 0:q], x[:, 3 * q:4 * q], x[:, 2 * q:3 * q]], axis=1)
    x = x * cos[None] + partner * sin_signed[None]
    return x.reshape(xT.shape)


def _inproj_kernel(x_ref, g_ref, wqkv_ref, wlru_ref, qg_ref, kg_ref, cos_ref, sin_ref,
                   qT_ref, k_ref, vT_ref, lin_ref, lgate_ref):
    w = lin_ref.shape[2]

    def project(j):
        rows = slice(j * KV_CHUNK, (j + 1) * KV_CHUNK)
        xn = _rms_rows(x_ref[0, rows, :], g_ref[...]).astype(BF16)
        qkvT = lax.dot_general(wqkv_ref[...], xn, (((1,), (1,)), ((), ())),
                               preferred_element_type=F32)
        lru = jnp.dot(xn, wlru_ref[...], preferred_element_type=F32)
        lin_ref[0, rows, :] = lru[:, :w]
        lgate_ref[0, rows, :] = lru[:, w:]
        return qkvT

    def finish(j, qkvT):
        rows = slice(j * KV_CHUNK, (j + 1) * KV_CHUNK)
        cos = cos_ref[:, rows]
        sin = sin_ref[:, rows]
        qT = _head_norm_rope(qkvT[:ATTN_WIDTH], qg_ref[...], cos, sin) * (HEAD_DIM ** -0.5 * LOG2_E)
        qT_ref[0, :, rows] = qT.astype(BF16)
        kT = _head_norm_rope(qkvT[ATTN_WIDTH:ATTN_WIDTH + KV_WIDTH], kg_ref[...], cos, sin)
        k_ref[0, rows, :] = kT.T.astype(BF16)
        vT_ref[0, j] = qkvT[ATTN_WIDTH + KV_WIDTH:].astype(BF16)

    pieces = vT_ref.shape[1]
    pending = project(0)
    for j in range(pieces):
        current = pending
        if j + 1 < pieces:
            pending = project(j + 1)
        finish(j, current)


def _inproj(x, g, wqkvT, wlru, qg_col, kg_col, cosT, sinT, tm):
    B, S, D = x.shape
    lw = wlru.shape[1] // 2
    nt = S // tm
    const = lambda b, i: (0, 0)
    return pl.pallas_call(
        _inproj_kernel,
        grid=(B, nt),
        in_specs=[
            pl.BlockSpec((1, tm, D), lambda b, i: (b, i, 0)),
            pl.BlockSpec((1, D), const),
            pl.BlockSpec(wqkvT.shape, const),
            pl.BlockSpec(wlru.shape, const),
            pl.BlockSpec((HEAD_DIM, 1), const),
            pl.BlockSpec((HEAD_DIM, 1), const),
            pl.BlockSpec((HEAD_DIM, tm), lambda b, i: (0, i)),
            pl.BlockSpec((HEAD_DIM, tm), lambda b, i: (0, i)),
        ],
        out_specs=[
            pl.BlockSpec((1, ATTN_WIDTH, tm), lambda b, i: (b, 0, i)),
            pl.BlockSpec((1, tm, KV_WIDTH), lambda b, i: (b, i, 0)),
            pl.BlockSpec((1, tm // KV_CHUNK, KV_WIDTH, KV_CHUNK), lambda b, i: (b, i, 0, 0)),
            pl.BlockSpec((1, tm, lw), lambda b, i: (b, i, 0)),
            pl.BlockSpec((1, tm, lw), lambda b, i: (b, i, 0)),
        ],
        out_shape=[
            jax.ShapeDtypeStruct((B, ATTN_WIDTH, S), BF16),
            jax.ShapeDtypeStruct((B, S, KV_WIDTH), BF16),
            jax.ShapeDtypeStruct((B, S // KV_CHUNK, KV_WIDTH, KV_CHUNK), BF16),
            jax.ShapeDtypeStruct((B, S, lw), F32),
            jax.ShapeDtypeStruct((B, S, lw), F32),
        ],
        compiler_params=pltpu.CompilerParams(
            dimension_semantics=("parallel", "parallel"), vmem_limit_bytes=VMEM_LIMIT_BYTES),
        name="inproj",
    )(x, g, wqkvT, wlru, qg_col, kg_col, cosT, sinT)


def _gelu_tanh(x):
    return 0.5 * x * (1.0 + jnp.tanh(np.sqrt(2.0 / np.pi).astype(np.float32) * (x + 0.044715 * (x * x * x))))


def _lru_kernel(xm_ref, x_ref, cw_ref, cb_ref, wg_ref, bg_ref, lam_ref, y_ref,
                X, Af, Uf, Ab, Ub, *, L, T, TY, K):
    S = L - N_META
    H = CONV_HALO
    seg = L // (SCAN_SEGMENTS * SCAN_GROUPS)
    zeros = jnp.zeros((H, LRU_LANES), F32)
    X[0:H] = zeros
    X[H + L:H + L + H] = zeros
    X[H:H + N_META] = xm_ref[...]
    X[H + N_META:H + L] = x_ref[0]

    lam = lam_ref[...]
    z = -lam
    softplus = jnp.maximum(z, 0.0) + jnp.log1p(jnp.exp(-jnp.abs(z)))
    half_decay = (0.5 * LRU_C) * softplus
    cw = cw_ref[...]
    cb = cb_ref[...]
    bg = bg_ref[...]

    def gate_chunk(i, carry):
        r0 = pl.multiple_of(i * T, SUBLANES)
        c = cb
        for j in range(CONV_W):
            c = c + X[pl.ds(r0 + (H - CONV_LEFT + j), T), :] * cw[j:j + 1]
        gh = jnp.dot(c.astype(BF16), wg_ref[...], preferred_element_type=F32) + bg
        for d, (A, U) in enumerate(((Af, Uf), (Ab, Ub))):
            base = 2 * d * LRU_LANES
            t_r = jnp.tanh(gh[:, base:base + LRU_LANES])
            t_i = jnp.tanh(gh[:, base + LRU_LANES:base + 2 * LRU_LANES])
            hd = half_decay[d:d + 1]
            nla = hd * t_r + hd
            a = jnp.exp2(nla * (-LOG2_E))
            x = jnp.tanh(nla) * (1.0 + a * a)
            mult = jnp.where(x == 0.0, 0.0, x * lax.rsqrt(x))
            A[pl.ds(r0, T), :] = a
            U[pl.ds(r0, T), :] = (mult * c) * (0.5 * t_i + 0.5)
        return carry

    lax.fori_loop(0, L // T, gate_chunk, 0)

    G = SCAN_GROUPS

    def seg_rows(g, j):
        return pl.ds(g * SCAN_SEGMENTS * seg + j, SCAN_SEGMENTS, stride=seg)

    def scan_block(jb, carry):
        state = [list(c) for c in carry]
        j0 = jb * K
        fwd = [[(Af[seg_rows(g, j0 + k), :], Uf[seg_rows(g, j0 + k), :]) for k in range(K)] for g in range(G)]
        bwd = [[(Ab[seg_rows(g, seg - 1 - j0 - k), :], Ub[seg_rows(g, seg - 1 - j0 - k), :])
                for k in range(K)] for g in range(G)]
        out = []
        for k in range(K):
            for g in range(G):
                hf, pf, hb, pb = state[g]
                a, u = fwd[g][k]
                hf = a * hf + u
                pf = a * pf
                a, u = bwd[g][k]
                hb = a * hb + u
                pb = a * pb
                state[g] = [hf, pf, hb, pb]
                out.append((g, k, hf, pf, hb, pb))
        for g, k, hf, pf, hb, pb in out:
            Uf[seg_rows(g, j0 + k), :] = hf
            Af[seg_rows(g, j0 + k), :] = pf
            Ub[seg_rows(g, seg - 1 - j0 - k), :] = hb
            Ab[seg_rows(g, seg - 1 - j0 - k), :] = pb
        return tuple(tuple(c) for c in state)

    zero = jnp.zeros((SCAN_SEGMENTS, LRU_LANES), F32)
    one = jnp.ones((SCAN_SEGMENTS, LRU_LANES), F32)
    ends = lax.fori_loop(0, seg // K, scan_block, tuple((zero, one, zero, one) for _ in range(G)))

    order = [(g, r) for g in range(G) for r in range(SCAN_SEGMENTS)]
    row = jnp.zeros((1, LRU_LANES), F32)
    rows = {}
    for g, r in order:
        rows[g, r] = row
        row = ends[g][0][r:r + 1] + ends[g][1][r:r + 1] * row
    cin_f = [jnp.concatenate([rows[g, r] for r in range(SCAN_SEGMENTS)], axis=0) for g in range(G)]
    row = jnp.zeros((1, LRU_LANES), F32)
    rows = {}
    for g, r in reversed(order):
        rows[g, r] = row
        row = ends[g][2][r:r + 1] + ends[g][3][r:r + 1] * row
    cin_b = [jnp.concatenate([rows[g, r] for r in range(SCAN_SEGMENTS)], axis=0) for g in range(G)]

    def fix_block(jb, carry):
        j0 = jb * K
        h = [(g, k, Uf[seg_rows(g, j0 + k), :] + Af[seg_rows(g, j0 + k), :] * cin_f[g]
              + Ub[seg_rows(g, j0 + k), :] + Ab[seg_rows(g, j0 + k), :] * cin_b[g])
             for k in range(K) for g in range(G)]
        for g, k, v in h:
            Uf[seg_rows(g, j0 + k), :] = v
        return carry

    lax.fori_loop(0, seg // K, fix_block, 0)

    def out_chunk(i, carry):
        r0 = pl.multiple_of(i * TY, SUBLANES)
        y_ref[0, pl.ds(r0, TY), :] = Uf[pl.ds(N_META + r0, TY), :]
        return carry

    lax.fori_loop(0, S // TY, out_chunk, 0)


def _pick_chunk(n, align, cap):
    best = align
    for t in range(align, cap + 1, align):
        if n % t == 0:
            best = t
    return best


def _lru(xin_meta, xin, cw, cb, wg, bg, lam):
    B, S, W = xin.shape
    L = S + N_META
    assert L % (SCAN_SEGMENTS * SCAN_GROUPS) == 0 and W % LRU_LANES == 0
    ncg = W // LRU_LANES
    T = _pick_chunk(L, BF16_ROWS, 512)
    TY = _pick_chunk(S, SUBLANES, 512)
    assert L % T == 0 and S % TY == 0
    scratch_rows = L + 2 * CONV_HALO
    K = _pick_chunk(L // (SCAN_SEGMENTS * SCAN_GROUPS), 1, SCAN_UNROLL_MAX)
    kern = functools.partial(_lru_kernel, L=L, T=T, TY=TY, K=K)
    return pl.pallas_call(
        kern,
        grid=(B, ncg),
        in_specs=[
            pl.BlockSpec((N_META, LRU_LANES), lambda b, c: (0, c)),
            pl.BlockSpec((1, S, LRU_LANES), lambda b, c: (b, 0, c)),
            pl.BlockSpec((CONV_W, LRU_LANES), lambda b, c: (0, c)),
            pl.BlockSpec((1, LRU_LANES), lambda b, c: (0, c)),
            pl.BlockSpec((None, LRU_LANES, 4 * LRU_LANES), lambda b, c: (c, 0, 0)),
            pl.BlockSpec((None, 1, 4 * LRU_LANES), lambda b, c: (c, 0, 0)),
            pl.BlockSpec((2, LRU_LANES), lambda b, c: (0, c)),
        ],
        out_specs=pl.BlockSpec((1, S, LRU_LANES), lambda b, c: (b, 0, c)),
        out_shape=jax.ShapeDtypeStruct((B, S, W), F32),
        scratch_shapes=[pltpu.VMEM((scratch_rows, LRU_LANES), F32)]
        + [pltpu.VMEM((L, LRU_LANES), F32) for _ in range(4)],
        compiler_params=pltpu.CompilerParams(
            dimension_semantics=("parallel", "parallel"), vmem_limit_bytes=VMEM_LIMIT_BYTES),
        name="lru",
    )(xin_meta, xin, cw, cb, wg, bg, lam)


def _attn_kernel(qT_ref, k_ref, vT_ref, km_ref, vTm_ref, og_ref, o_ref,
                 qext, m_s, acc_s, s_carry, smax_carry):
    tq = qT_ref.shape[2]
    nchunks = vT_ref.shape[1]
    zeros = jnp.zeros((HEAD_DIM, tq), BF16)
    for h in range(N_Q_HEADS):
        qh = qT_ref[0, h * HEAD_DIM:(h + 1) * HEAD_DIM, :]
        parts = [zeros] * N_KV_HEADS
        parts[h // Q_PER_KV] = qh
        qext[h] = jnp.concatenate(parts, axis=0)

    def with_ones(vT_all, g):
        ones = jnp.ones((BF16_ROWS, vT_all.shape[1]), BF16)
        return jnp.concatenate([vT_all[g * HEAD_DIM:(g + 1) * HEAD_DIM], ones], axis=0)

    def scores(c, h):
        kc = k_ref[0, pl.ds(pl.multiple_of(c * KV_CHUNK, KV_CHUNK), KV_CHUNK), :]
        return jnp.dot(kc, qext[h], preferred_element_type=F32)

    def init_from_meta():
        km = km_ref[...]
        vTm = [with_ones(vTm_ref[...], g) for g in range(N_KV_HEADS)]
        s_meta = [jnp.dot(km, qext[h], preferred_element_type=F32) for h in range(N_Q_HEADS)]
        p_meta = []
        for h in range(N_Q_HEADS):
            m = jnp.max(s_meta[h], axis=0, keepdims=True)
            m_s[h] = m
            p_meta.append(jnp.exp2(s_meta[h] - m).astype(BF16))
        for h in range(N_Q_HEADS):
            acc_s[h] = jnp.dot(vTm[h // Q_PER_KV], p_meta[h], preferred_element_type=F32)[:ACC_ROWS]

    for n in range(QK_AHEAD):
        first = scores(n // N_Q_HEADS, n % N_Q_HEADS)
        s_carry[n] = first
        smax_carry[n] = jnp.max(first, axis=0, keepdims=True)
    init_from_meta()

    def values(c):
        return [with_ones(vT_ref[0, c], g) for g in range(N_KV_HEADS)]

    def exact_chunk(c, carry):
        vc = values(c)
        for h in range(N_Q_HEADS):
            s = scores(c, h)
            m_old = m_s[h]
            m_new = jnp.maximum(m_old, jnp.max(s, axis=0, keepdims=True))
            p = jnp.exp2(s - m_new)
            pv = jnp.dot(vc[h // Q_PER_KV], p.astype(BF16), preferred_element_type=F32)
            acc_s[h] = jnp.exp2(m_old - m_new) * acc_s[h] + pv[:ACC_ROWS]
            m_s[h] = m_new
        return carry

    def fast_trip(i, excess):
        vcs = [values(i * CHUNKS_PER_STEP + j) for j in range(CHUNKS_PER_STEP)]
        units = [(j, h) for j in range(CHUNKS_PER_STEP) for h in range(N_Q_HEADS)]
        m_start = [m_s[h] for h in range(N_Q_HEADS)]
        pending = [(s_carry[n], smax_carry[n]) for n in range(QK_AHEAD)]
        for n in range(QK_AHEAD):
            excess = jnp.maximum(excess, pending[n][1] - m_start[units[n][1]])
        for n, (j, h) in enumerate(units):
            s, s_max = pending.pop(0)
            if n + QK_AHEAD < len(units):
                ja, ha = units[n + QK_AHEAD]
                ahead = scores(i * CHUNKS_PER_STEP + ja, ha)
                ahead_max = jnp.max(ahead, axis=0, keepdims=True)
                excess = jnp.maximum(excess, ahead_max - m_start[ha])
                pending.append((ahead, ahead_max))
            else:
                nxt = n + QK_AHEAD - len(units)
                c_next = jnp.minimum((i + 1) * CHUNKS_PER_STEP + units[nxt][0], nchunks - 1)
                ahead = scores(c_next, units[nxt][1])
                s_carry[nxt] = ahead
                smax_carry[nxt] = jnp.max(ahead, axis=0, keepdims=True)
            m_ref = m_s[h]
            p = jnp.exp2(s - m_ref)
            pv = jnp.dot(vcs[j][h // Q_PER_KV], p.astype(BF16), preferred_element_type=F32)
            m_new = jnp.maximum(m_ref, s_max)
            acc_s[h] = (acc_s[h] + pv[:ACC_ROWS]) * jnp.exp2(m_ref - m_new)
            m_s[h] = m_new
        return excess

    assert nchunks % CHUNKS_PER_STEP == 0
    excess = lax.fori_loop(0, nchunks // CHUNKS_PER_STEP, fast_trip, jnp.full((1, tq), -jnp.inf, F32),
                           unroll=4)

    @pl.when(jnp.logical_not(jnp.max(excess) <= SAFE_EXCESS))
    def _():
        init_from_meta()
        lax.fori_loop(0, nchunks, exact_chunk, 0)

    oT = jnp.concatenate([acc_s[h, :HEAD_DIM] / acc_s[h, HEAD_DIM:HEAD_DIM + 1]
                          for h in range(N_Q_HEADS)], axis=0)
    ms = jnp.mean(oT * oT, axis=0, keepdims=True)
    oT = oT * lax.rsqrt(ms + EPS) * og_ref[...]
    o_ref[0] = oT.T.astype(BF16)


def _attention(qT, k, vT, k_meta, vT_meta, og_col, tq):
    B, _, S = qT.shape
    nchunks = S // KV_CHUNK
    return pl.pallas_call(
        _attn_kernel,
        grid=(B, S // tq),
        in_specs=[
            pl.BlockSpec((1, ATTN_WIDTH, tq), lambda b, i: (b, 0, i)),
            pl.BlockSpec((1, S, KV_WIDTH), lambda b, i: (b, 0, 0)),
            pl.BlockSpec((1, nchunks, KV_WIDTH, KV_CHUNK), lambda b, i: (b, 0, 0, 0)),
            pl.BlockSpec((N_META, KV_WIDTH), lambda b, i: (0, 0)),
            pl.BlockSpec((KV_WIDTH, N_META), lambda b, i: (0, 0)),
            pl.BlockSpec((ATTN_WIDTH, 1), lambda b, i: (0, 0)),
        ],
        out_specs=pl.BlockSpec((1, tq, ATTN_WIDTH), lambda b, i: (b, i, 0)),
        out_shape=jax.ShapeDtypeStruct((B, S, ATTN_WIDTH), BF16),
        scratch_shapes=[
            pltpu.VMEM((N_Q_HEADS, KV_WIDTH, tq), BF16),
            pltpu.VMEM((N_Q_HEADS, 1, tq), F32),
            pltpu.VMEM((N_Q_HEADS, ACC_ROWS, tq), F32),
            pltpu.VMEM((QK_AHEAD, KV_CHUNK, tq), F32),
            pltpu.VMEM((QK_AHEAD, 1, tq), F32),
        ],
        compiler_params=pltpu.CompilerParams(
            dimension_semantics=("parallel", "parallel"), vmem_limit_bytes=VMEM_LIMIT_BYTES),
        name="attention",
    )(qT, k, vT, k_meta, vT_meta, og_col)


def _post_kernel(x_ref, ao_ref, ly_ref, lgate_ref, lg_ref, woa_ref, wol_ref, fg_ref, wg_ref, wu_ref, wd_ref,
                 og_ref, o_ref, *, n_ffn_chunks):
    tm = x_ref.shape[1]
    pieces = [slice(j * (tm // POST_PIECES), (j + 1) * (tm // POST_PIECES)) for j in range(POST_PIECES)]

    def mix(rows):
        lo = _rms_rows(ly_ref[0, rows, :] * _gelu_tanh(lgate_ref[0, rows, :]), lg_ref[...]).astype(BF16)
        return (x_ref[0, rows, :]
                + jnp.dot(ao_ref[0, rows, :], woa_ref[...], preferred_element_type=F32)
                + jnp.dot(lo, wol_ref[...], preferred_element_type=F32))

    def ffn(rows, h):
        xn = _rms_rows(h, fg_ref[...]).astype(BF16)
        fw = wg_ref.shape[1] // n_ffn_chunks
        for c in range(n_ffn_chunks):
            gate = jnp.dot(xn, wg_ref[:, c * fw:(c + 1) * fw], preferred_element_type=F32)
            up = jnp.dot(xn, wu_ref[:, c * fw:(c + 1) * fw], preferred_element_type=F32)
            act = (gate * jax.nn.sigmoid(gate) * up).astype(BF16)
            h = h + jnp.dot(act, wd_ref[c * fw:(c + 1) * fw, :], preferred_element_type=F32)
        o_ref[0, rows, :] = _rms_rows(h, og_ref[...])

    hs = [mix(rows) for rows in pieces]
    for rows, h in zip(pieces, hs):
        ffn(rows, h)


def _post(x, attn_o, lru_y, lru_gate, lru_g, wo_a, wo_l, ffn_g, w_gate, w_up, w_down, out_g, tm):
    B, S, D = x.shape
    F = w_gate.shape[1]
    n_ffn_chunks = 1
    const = lambda b, i: (0, 0)
    resident = functools.partial(pl.BlockSpec, index_map=const, pipeline_mode=pl.Buffered(1))
    row = lambda w: pl.BlockSpec((1, tm, w), lambda b, i: (b, i, 0))
    return pl.pallas_call(
        functools.partial(_post_kernel, n_ffn_chunks=n_ffn_chunks),
        grid=(B, S // tm),
        in_specs=[
            row(D), row(ATTN_WIDTH), row(lru_y.shape[2]), row(lru_y.shape[2]),
            resident((1, lru_y.shape[2])),
            resident(wo_a.shape), resident(wo_l.shape),
            resident((1, D)),
            resident(w_gate.shape), resident(w_up.shape), resident(w_down.shape),
            resident((1, D)),
        ],
        out_specs=row(D),
        out_shape=jax.ShapeDtypeStruct((B, S, D), F32),
        compiler_params=pltpu.CompilerParams(
            dimension_semantics=("parallel", "parallel"), vmem_limit_bytes=VMEM_LIMIT_BYTES),
        name="post",
    )(x, attn_o, lru_y, lru_gate, lru_g, wo_a, wo_l, ffn_g, w_gate, w_up, w_down, out_g)


def _rope_tables(n_tokens):
    n_rows = n_tokens // GRID_W
    row = jnp.repeat(jnp.arange(n_rows), GRID_W).astype(F32)
    col = jnp.tile(jnp.arange(GRID_W), n_rows).astype(F32)
    freqs = ROPE_THETA ** (-jnp.arange(0, ROPE_AXIS_DIM, 2, dtype=F32) / ROPE_AXIS_DIM)
    ar = freqs[:, None] * row[None, :]
    ac = freqs[:, None] * col[None, :]
    cos = jnp.concatenate([jnp.cos(ar), jnp.cos(ar), jnp.cos(ac), jnp.cos(ac)], axis=0)
    sin = jnp.concatenate([-jnp.sin(ar), jnp.sin(ar), -jnp.sin(ac), jnp.sin(ac)], axis=0)
    return cos, sin


def _block_diag_pairs(w):
    nb, bw, _ = w.shape
    w = w.reshape(nb // 2, 2, bw, bw)
    z = jnp.zeros((nb // 2, bw, bw), w.dtype)
    top = jnp.concatenate([w[:, 0], z], axis=2)
    bot = jnp.concatenate([z, w[:, 1]], axis=2)
    return jnp.concatenate([top, bot], axis=1)


def _row_tile(S):
    return 512 if S % 512 == 0 else S


def _trunk(x, prep):
    B, S, D = x.shape
    tm = _row_tile(S)
    cosT, sinT = _rope_tables(S)
    qT, k, vT, lin, lgate = _inproj(x, prep["mix_g"], prep["wqkvT"], prep["wlru"], prep["qg"], prep["kg"],
                                    cosT, sinT, 2 * tm if S % (2 * tm) == 0 else tm)
    lru_y = _lru(prep["lin_meta"], lin, prep["conv_w"], prep["conv_b"], prep["w_gates"],
                 prep["b_gates"], prep["lam"])
    attn_o = _attention(qT, k, vT, prep["k_meta"], prep["vT_meta"], prep["attn_g"], tm)
    return _post(x, attn_o, lru_y, lgate, prep["lru_g"], prep["wo_a"], prep["wo_l"], prep["ffn_g"],
                 prep["w_gate"], prep["w_up"], prep["w_down"], prep["final_g"], tm)


def kernel(x_prompt, x_sample, meta_tokens, norm_mix_g, w_in, q_norm_g, k_norm_g, conv_w, conv_b, lru_w_a, lru_b_a, lru_w_x, lru_b_x, lru_lam, attn_out_g, lru_out_g, w_out, norm_ffn_g, w_gate_up, w_down, final_norm_g):
    assert w_in.shape[0] == 1, "single-layer trunk"
    D = x_prompt.shape[-1]
    lw = D - ATTN_WIDTH
    qkv_w = ATTN_WIDTH + 2 * KV_WIDTH
    w = w_in[0]
    prep = {
        "mix_g": norm_mix_g[0][None],
        "wqkvT": w[:, :qkv_w].T.astype(BF16),
        "wlru": w[:, qkv_w:].astype(BF16),
        "qg": q_norm_g[0][:, None],
        "kg": k_norm_g[0][:, None],
        "conv_w": conv_w[0],
        "conv_b": conv_b[0][None],
        "lam": lru_lam[0],
        "attn_g": attn_out_g[0][:, None],
        "lru_g": lru_out_g[0][None],
        "wo_a": w_out[0][:ATTN_WIDTH].astype(BF16),
        "wo_l": w_out[0][ATTN_WIDTH:].astype(BF16),
        "ffn_g": norm_ffn_g[0][None],
        "w_gate": w_gate_up[0][:, :w_gate_up.shape[2] // 2].astype(BF16),
        "w_up": w_gate_up[0][:, w_gate_up.shape[2] // 2:].astype(BF16),
        "w_down": w_down[0].astype(BF16),
        "final_g": final_norm_g[None],
    }
    wa = [_block_diag_pairs(lru_w_a[0, d]) for d in range(2)]
    wx = [_block_diag_pairs(lru_w_x[0, d]) for d in range(2)]
    prep["w_gates"] = (0.5 * jnp.concatenate([wa[0], wx[0], wa[1], wx[1]], axis=2)).astype(BF16)
    ncg = lw // LRU_LANES
    bias = [b.reshape(ncg, 1, LRU_LANES) for b in (lru_b_a[0, 0], lru_b_x[0, 0], lru_b_a[0, 1], lru_b_x[0, 1])]
    prep["b_gates"] = 0.5 * jnp.concatenate(bias, axis=2)

    meta = jnp.zeros((1, META_PAD, D), F32).at[0, :N_META].set(meta_tokens)
    ones = jnp.ones((HEAD_DIM, META_PAD), F32)
    _, k_m, vT_m, lin_m, _ = _inproj(meta, prep["mix_g"], prep["wqkvT"], prep["wlru"], prep["qg"], prep["kg"],
                                     ones, jnp.zeros_like(ones), META_PAD)
    prep["k_meta"] = k_m[0, :N_META]
    prep["vT_meta"] = vT_m[0, 0, :, :N_META]
    prep["lin_meta"] = lin_m[0, :N_META]
    return _trunk(x_prompt, prep), _trunk(x_sample, prep)
```

```python
import functools

import jax
import jax.numpy as jnp
import numpy as np
from jax import lax
from jax.experimental import pallas as pl
from jax.experimental.pallas import tpu as pltpu

F32 = jnp.float32
BF16 = jnp.bfloat16

N_META = 16
GRID_W = 64
HEAD_DIM = 64
N_Q_HEADS = 8
N_KV_HEADS = 2
Q_PER_KV = N_Q_HEADS // N_KV_HEADS
ATTN_WIDTH = N_Q_HEADS * HEAD_DIM
KV_WIDTH = N_KV_HEADS * HEAD_DIM
LRU_BLOCKS = 8
CONV_W = 4
CONV_LEFT = 2
LRU_C = 8.0
ROPE_AXIS_DIM = HEAD_DIM // 2
ROPE_THETA = 10000.0
EPS = 1e-6
LOG2_E = float(np.log2(np.e))

LANES = 128
SUBLANES = 8
BF16_ROWS = 16
VMEM_LIMIT_BYTES = 56 * 1024 * 1024

LRU_LANES = LANES
SCAN_SEGMENTS = SUBLANES
SCAN_GROUPS = 2
SCAN_UNROLL_MAX = 3
CONV_HALO = SUBLANES
KV_CHUNK = 256
CHUNKS_PER_STEP = 4
QK_AHEAD = 2
SAFE_EXCESS = 64.0
ACC_ROWS = HEAD_DIM + SUBLANES
POST_PIECES = 2
META_PAD = KV_CHUNK


def _rms_rows(x, g):
    ms = jnp.mean(x * x, axis=-1, keepdims=True)
    return x * lax.rsqrt(ms + EPS) * g


def _head_norm_rope(xT, g_col, cos, sin_signed):
    nh = xT.shape[0] // HEAD_DIM
    x = xT.reshape(nh, HEAD_DIM, xT.shape[1])
    ms = jnp.mean(x * x, axis=1, keepdims=True)
    x = x * lax.rsqrt(ms + EPS) * g_col[None]
    q = ROPE_AXIS_DIM // 2
    partner = jnp.concatenate(
        [x[:, q:2 * q], x[:, 0:q], x[:, 3 * q:4 * q], x[:, 2 * q:3 * q]], axis=1)
    x = x * cos[None] + partner * sin_signed[None]
    return x.reshape(xT.shape)


def _inproj_kernel(x_ref, g_ref, wqkv_ref, wlru_ref, qg_ref, kg_ref, cos_ref, sin_ref,
                   qT_ref, k_ref, vT_ref, lin_ref, lgate_ref):
    w = lin_ref.shape[2]

    def project(j):
        rows = slice(j * KV_CHUNK, (j + 1) * KV_CHUNK)
        xn = _rms_rows(x_ref[0, rows, :], g_ref[...]).astype(BF16)
        qkvT = lax.dot_general(wqkv_ref[...], xn, (((1,), (1,)), ((), ())),
                               preferred_element_type=F32)
        lru = jnp.dot(xn, wlru_ref[...], preferred_element_type=F32)
        lin_ref[0, rows, :] = lru[:, :w]
        lgate_ref[0, rows, :] = lru[:, w:]
        return qkvT

    def finish(j, qkvT):
        rows = slice(j * KV_CHUNK, (j + 1) * KV_CHUNK)
        cos = cos_ref[:, rows]
        sin = sin_ref[:, rows]
        qT = _head_norm_rope(qkvT[:ATTN_WIDTH], qg_ref[...], cos, sin) * (HEAD_DIM ** -0.5 * LOG2_E)
        qT_ref[0, :, rows] = qT.astype(BF16)
        kT = _head_norm_rope(qkvT[ATTN_WIDTH:ATTN_WIDTH + KV_WIDTH], kg_ref[...], cos, sin)
        k_ref[0, rows, :] = kT.T.astype(BF16)
        vT_ref[0, j] = qkvT[ATTN_WIDTH + KV_WIDTH:].astype(BF16)

    pieces = vT_ref.shape[1]
    pending = project(0)
    for j in range(pieces):
        current = pending
        if j + 1 < pieces:
            pending = project(j + 1)
        finish(j, current)


def _inproj(x, g, wqkvT, wlru, qg_col, kg_col, cosT, sinT, tm):
    B, S, D = x.shape
    lw = wlru.shape[1] // 2
    nt = S // tm
    const = lambda b, i: (0, 0)
    return pl.pallas_call(
        _inproj_kernel,
        grid=(B, nt),
        in_specs=[
            pl.BlockSpec((1, tm, D), lambda b, i: (b, i, 0)),
            pl.BlockSpec((1, D), const),
            pl.BlockSpec(wqkvT.shape, const),
            pl.BlockSpec(wlru.shape, const),
            pl.BlockSpec((HEAD_DIM, 1), const),
            pl.BlockSpec((HEAD_DIM, 1), const),
            pl.BlockSpec((HEAD_DIM, tm), lambda b, i: (0, i)),
            pl.BlockSpec((HEAD_DIM, tm), lambda b, i: (0, i)),
        ],
        out_specs=[
            pl.BlockSpec((1, ATTN_WIDTH, tm), lambda b, i: (b, 0, i)),
            pl.BlockSpec((1, tm, KV_WIDTH), lambda b, i: (b, i, 0)),
            pl.BlockSpec((1, tm // KV_CHUNK, KV_WIDTH, KV_CHUNK), lambda b, i: (b, i, 0, 0)),
            pl.BlockSpec((1, tm, lw), lambda b, i: (b, i, 0)),
            pl.BlockSpec((1, tm, lw), lambda b, i: (b, i, 0)),
        ],
        out_shape=[
            jax.ShapeDtypeStruct((B, ATTN_WIDTH, S), BF16),
            jax.ShapeDtypeStruct((B, S, KV_WIDTH), BF16),
            jax.ShapeDtypeStruct((B, S // KV_CHUNK, KV_WIDTH, KV_CHUNK), BF16),
            jax.ShapeDtypeStruct((B, S, lw), F32),
            jax.ShapeDtypeStruct((B, S, lw), F32),
        ],
        compiler_params=pltpu.CompilerParams(
            dimension_semantics=("parallel", "parallel"), vmem_limit_bytes=VMEM_LIMIT_BYTES),
        name="inproj",
    )(x, g, wqkvT, wlru, qg_col, kg_col, cosT, sinT)


def _gelu_tanh(x):
    return 0.5 * x * (1.0 + jnp.tanh(np.sqrt(2.0 / np.pi).astype(np.float32) * (x + 0.044715 * (x * x * x))))


def _lru_kernel(xm_ref, x_ref, cw_ref, cb_ref, wg_ref, bg_ref, lam_ref, y_ref,
                X, Af, Uf, Ab, Ub, *, L, T, TY, K):
    S = L - N_META
    H = CONV_HALO
    seg = L // (SCAN_SEGMENTS * SCAN_GROUPS)
    zeros = jnp.zeros((H, LRU_LANES), F32)
    X[0:H] = zeros
    X[H + L:H + L + H] = zeros
    X[H:H + N_META] = xm_ref[...]
    X[H + N_META:H + L] = x_ref[0]

    lam = lam_ref[...]
    z = -lam
    softplus = jnp.maximum(z, 0.0) + jnp.log1p(jnp.exp(-jnp.abs(z)))
    half_decay = (0.5 * LRU_C) * softplus
    cw = cw_ref[...]
    cb = cb_ref[...]
    bg = bg_ref[...]

    def gate_chunk(i, carry):
        r0 = pl.multiple_of(i * T, SUBLANES)
        c = cb
        for j in range(CONV_W):
            c = c + X[pl.ds(r0 + (H - CONV_LEFT + j), T), :] * cw[j:j + 1]
        gh = jnp.dot(c.astype(BF16), wg_ref[...], preferred_element_type=F32) + bg
        for d, (A, U) in enumerate(((Af, Uf), (Ab, Ub))):
            base = 2 * d * LRU_LANES
            t_r = jnp.tanh(gh[:, base:base + LRU_LANES])
            t_i = jnp.tanh(gh[:, base + LRU_LANES:base + 2 * LRU_LANES])
            hd = half_decay[d:d + 1]
            nla = hd * t_r + hd
            a = jnp.exp2(nla * (-LOG2_E))
            x = jnp.tanh(nla) * (1.0 + a * a)
            mult = jnp.where(x == 0.0, 0.0, x * lax.rsqrt(x))
            A[pl.ds(r0, T), :] = a
            U[pl.ds(r0, T), :] = (mult * c) * (0.5 * t_i + 0.5)
        return carry

    lax.fori_loop(0, L // T, gate_chunk, 0)

    G = SCAN_GROUPS

    def seg_rows(g, j):
        return pl.ds(g * SCAN_SEGMENTS * seg + j, SCAN_SEGMENTS, stride=seg)

    def scan_block(jb, carry):
        state = [list(c) for c in carry]
        j0 = jb * K
        fwd = [[(Af[seg_rows(g, j0 + k), :], Uf[seg_rows(g, j0 + k), :]) for k in range(K)] for g in range(G)]
        bwd = [[(Ab[seg_rows(g, seg - 1 - j0 - k), :], Ub[seg_rows(g, seg - 1 - j0 - k), :])
                for k in range(K)] for g in range(G)]
        out = []
        for k in range(K):
            for g in range(G):
                hf, pf, hb, pb = state[g]
                a, u = fwd[g][k]
                hf = a * hf + u
                pf = a * pf
                a, u = bwd[g][k]
                hb = a * hb + u
                pb = a * pb
                state[g] = [hf, pf, hb, pb]
                out.append((g, k, hf, pf, hb, pb))
        for g, k, hf, pf, hb, pb in out:
            Uf[seg_rows(g, j0 + k), :] = hf
            Af[seg_rows(g, j0 + k), :] = pf
            Ub[seg_rows(g, seg - 1 - j0 - k), :] = hb
            Ab[seg_rows(g, seg - 1 - j0 - k), :] = pb
        return tuple(tuple(c) for c in state)

    zero = jnp.zeros((SCAN_SEGMENTS, LRU_LANES), F32)
    one = jnp.ones((SCAN_SEGMENTS, LRU_LANES), F32)
    ends = lax.fori_loop(0, seg // K, scan_block, tuple((zero, one, zero, one) for _ in range(G)))

    order = [(g, r) for g in range(G) for r in range(SCAN_SEGMENTS)]
    row = jnp.zeros((1, LRU_LANES), F32)
    rows = {}
    for g, r in order:
        rows[g, r] = row
        row = ends[g][0][r:r + 1] + ends[g][1][r:r + 1] * row
    cin_f = [jnp.concatenate([rows[g, r] for r in range(SCAN_SEGMENTS)], axis=0) for g in range(G)]
    row = jnp.zeros((1, LRU_LANES), F32)
    rows = {}
    for g, r in reversed(order):
        rows[g, r] = row
        row = ends[g][2][r:r + 1] + ends[g][3][r:r + 1] * row
    cin_b = [jnp.concatenate([rows[g, r] for r in range(SCAN_SEGMENTS)], axis=0) for g in range(G)]

    def fix_block(jb, carry):
        j0 = jb * K
        h = [(g, k, Uf[seg_rows(g, j0 + k), :] + Af[seg_rows(g, j0 + k), :] * cin_f[g]
              + Ub[seg_rows(g, j0 + k), :] + Ab[seg_rows(g, j0 + k), :] * cin_b[g])
             for k in range(K) for g in range(G)]
        for g, k, v in h:
            Uf[seg_rows(g, j0 + k), :] = v
        return carry

    lax.fori_loop(0, seg // K, fix_block, 0)

    def out_chunk(i, carry):
        r0 = pl.multiple_of(i * TY, SUBLANES)
        y_ref[0, pl.ds(r0, TY), :] = Uf[pl.ds(N_META + r0, TY), :]
        return carry

    lax.fori_loop(0, S // TY, out_chunk, 0)


def _pick_chunk(n, align, cap):
    best = align
    for t in range(align, cap + 1, align):
        if n % t == 0:
            best = t
    return best


def _lru(xin_meta, xin, cw, cb, wg, bg, lam):
    B, S, W = xin.shape
    L = S + N_META
    assert L % (SCAN_SEGMENTS * SCAN_GROUPS) == 0 and W % LRU_LANES == 0
    ncg = W // LRU_LANES
    T = _pick_chunk(L, BF16_ROWS, 512)
    TY = _pick_chunk(S, SUBLANES, 512)
    assert L % T == 0 and S % TY == 0
    scratch_rows = L + 2 * CONV_HALO
    K = _pick_chunk(L // (SCAN_SEGMENTS * SCAN_GROUPS), 1, SCAN_UNROLL_MAX)
    kern = functools.partial(_lru_kernel, L=L, T=T, TY=TY, K=K)
    return pl.pallas_call(
        kern,
        grid=(B, ncg),
        in_specs=[
            pl.BlockSpec((N_META, LRU_LANES), lambda b, c: (0, c)),
            pl.BlockSpec((1, S, LRU_LANES), lambda b, c: (b, 0, c)),
            pl.BlockSpec((CONV_W, LRU_LANES), lambda b, c: (0, c)),
            pl.BlockSpec((1, LRU_LANES), lambda b, c: (0, c)),
            pl.BlockSpec((None, LRU_LANES, 4 * LRU_LANES), lambda b, c: (c, 0, 0)),
            pl.BlockSpec((None, 1, 4 * LRU_LANES), lambda b, c: (c, 0, 0)),
            pl.BlockSpec((2, LRU_LANES), lambda b, c: (0, c)),
        ],
        out_specs=pl.BlockSpec((1, S, LRU_LANES), lambda b, c: (b, 0, c)),
        out_shape=jax.ShapeDtypeStruct((B, S, W), F32),
        scratch_shapes=[pltpu.VMEM((scratch_rows, LRU_LANES), F32)]
        + [pltpu.VMEM((L, LRU_LANES), F32) for _ in range(4)],
        compiler_params=pltpu.CompilerParams(
            dimension_semantics=("parallel", "parallel"), vmem_limit_bytes=VMEM_LIMIT_BYTES),
        name="lru",
    )(xin_meta, xin, cw, cb, wg, bg, lam)


def _attn_kernel(qT_ref, k_ref, vT_ref, km_ref, vTm_ref, og_ref, o_ref,
                 qext, m_s, acc_s, s_carry, smax_carry):
    tq = qT_ref.shape[2]
    nchunks = vT_ref.shape[1]
    zeros = jnp.zeros((HEAD_DIM, tq), BF16)
    for h in range(N_Q_HEADS):
        qh = qT_ref[0, h * HEAD_DIM:(h + 1) * HEAD_DIM, :]
        parts = [zeros] * N_KV_HEADS
        parts[h // Q_PER_KV] = qh
        qext[h] = jnp.concatenate(parts, axis=0)

    def with_ones(vT_all, g):
        ones = jnp.ones((BF16_ROWS, vT_all.shape[1]), BF16)
        return jnp.concatenate([vT_all[g * HEAD_DIM:(g + 1) * HEAD_DIM], ones], axis=0)

    def scores(c, h):
        kc = k_ref[0, pl.ds(pl.multiple_of(c * KV_CHUNK, KV_CHUNK), KV_CHUNK), :]
        return jnp.dot(kc, qext[h], preferred_element_type=F32)

    def init_from_meta():
        km = km_ref[...]
        vTm = [with_ones(vTm_ref[...], g) for g in range(N_KV_HEADS)]
        s_meta = [jnp.dot(km, qext[h], preferred_element_type=F32) for h in range(N_Q_HEADS)]
        p_meta = []
        for h in range(N_Q_HEADS):
            m = jnp.max(s_meta[h], axis=0, keepdims=True)
            m_s[h] = m
            p_meta.append(jnp.exp2(s_meta[h] - m).astype(BF16))
        for h in range(N_Q_HEADS):
            acc_s[h] = jnp.dot(vTm[h // Q_PER_KV], p_meta[h], preferred_element_type=F32)[:ACC_ROWS]

    for n in range(QK_AHEAD):
        first = scores(n // N_Q_HEADS, n % N_Q_HEADS)
        s_carry[n] = first
        smax_carry[n] = jnp.max(first, axis=0, keepdims=True)
    init_from_meta()

    def values(c):
        return [with_ones(vT_ref[0, c], g) for g in range(N_KV_HEADS)]

    def exact_chunk(c, carry):
        vc = values(c)
        for h in range(N_Q_HEADS):
            s = scores(c, h)
            m_old = m_s[h]
            m_new = jnp.maximum(m_old, jnp.max(s, axis=0, keepdims=True))
            p = jnp.exp2(s - m_new)
            pv = jnp.dot(vc[h // Q_PER_KV], p.astype(BF16), preferred_element_type=F32)
            acc_s[h] = jnp.exp2(m_old - m_new) * acc_s[h] + pv[:ACC_ROWS]
            m_s[h] = m_new
        return carry

    def fast_trip(i, excess):
        vcs = [values(i * CHUNKS_PER_STEP + j) for j in range(CHUNKS_PER_STEP)]
        units = [(j, h) for j in range(CHUNKS_PER_STEP) for h in range(N_Q_HEADS)]
        m_start = [m_s[h] for h in range(N_Q_HEADS)]
        pending = [(s_carry[n], smax_carry[n]) for n in range(QK_AHEAD)]
        for n in range(QK_AHEAD):
            excess = jnp.maximum(excess, pending[n][1] - m_start[units[n][1]])
        for n, (j, h) in enumerate(units):
            s, s_max = pending.pop(0)
            if n + QK_AHEAD < len(units):
                ja, ha = units[n + QK_AHEAD]
                ahead = scores(i * CHUNKS_PER_STEP + ja, ha)
                ahead_max = jnp.max(ahead, axis=0, keepdims=True)
                excess = jnp.maximum(excess, ahead_max - m_start[ha])
                pending.append((ahead, ahead_max))
            else:
                nxt = n + QK_AHEAD - len(units)
                c_next = jnp.minimum((i + 1) * CHUNKS_PER_STEP + units[nxt][0], nchunks - 1)
                ahead = scores(c_next, units[nxt][1])
                s_carry[nxt] = ahead
                smax_carry[nxt] = jnp.max(ahead, axis=0, keepdims=True)
            m_ref = m_s[h]
            p = jnp.exp2(s - m_ref)
            pv = jnp.dot(vcs[j][h // Q_PER_KV], p.astype(BF16), preferred_element_type=F32)
            m_new = jnp.maximum(m_ref, s_max)
            acc_s[h] = (acc_s[h] + pv[:ACC_ROWS]) * jnp.exp2(m_ref - m_new)
            m_s[h] = m_new
        return excess

    assert nchunks % CHUNKS_PER_STEP == 0
    excess = lax.fori_loop(0, nchunks // CHUNKS_PER_STEP, fast_trip, jnp.full((1, tq), -jnp.inf, F32),
                           unroll=8)

    @pl.when(jnp.logical_not(jnp.max(excess) <= SAFE_EXCESS))
    def _():
        init_from_meta()
        lax.fori_loop(0, nchunks, exact_chunk, 0)

    oT = jnp.concatenate([acc_s[h, :HEAD_DIM] / acc_s[h, HEAD_DIM:HEAD_DIM + 1]
                          for h in range(N_Q_HEADS)], axis=0)
    ms = jnp.mean(oT * oT, axis=0, keepdims=True)
    oT = oT * lax.rsqrt(ms + EPS) * og_ref[...]
    o_ref[0] = oT.T.astype(BF16)


def _attention(qT, k, vT, k_meta, vT_meta, og_col, tq):
    B, _, S = qT.shape
    nchunks = S // KV_CHUNK
    return pl.pallas_call(
        _attn_kernel,
        grid=(B, S // tq),
        in_specs=[
            pl.BlockSpec((1, ATTN_WIDTH, tq), lambda b, i: (b, 0, i)),
            pl.BlockSpec((1, S, KV_WIDTH), lambda b, i: (b, 0, 0)),
            pl.BlockSpec((1, nchunks, KV_WIDTH, KV_CHUNK), lambda b, i: (b, 0, 0, 0)),
            pl.BlockSpec((N_META, KV_WIDTH), lambda b, i: (0, 0)),
            pl.BlockSpec((KV_WIDTH, N_META), lambda b, i: (0, 0)),
            pl.BlockSpec((ATTN_WIDTH, 1), lambda b, i: (0, 0)),
        ],
        out_specs=pl.BlockSpec((1, tq, ATTN_WIDTH), lambda b, i: (b, i, 0)),
        out_shape=jax.ShapeDtypeStruct((B, S, ATTN_WIDTH), BF16),
        scratch_shapes=[
            pltpu.VMEM((N_Q_HEADS, KV_WIDTH, tq), BF16),
            pltpu.VMEM((N_Q_HEADS, 1, tq), F32),
            pltpu.VMEM((N_Q_HEADS, ACC_ROWS, tq), F32),
            pltpu.VMEM((QK_AHEAD, KV_CHUNK, tq), F32),
            pltpu.VMEM((QK_AHEAD, 1, tq), F32),
        ],
        compiler_params=pltpu.CompilerParams(
            dimension_semantics=("parallel", "parallel"), vmem_limit_bytes=VMEM_LIMIT_BYTES),
        name="attention",
    )(qT, k, vT, k_meta, vT_meta, og_col)


def _post_kernel(x_ref, ao_ref, ly_ref, lgate_ref, lg_ref, woa_ref, wol_ref, fg_ref, wg_ref, wu_ref, wd_ref,
                 og_ref, o_ref, *, n_ffn_chunks):
    tm = x_ref.shape[1]
    pieces = [slice(j * (tm // POST_PIECES), (j + 1) * (tm // POST_PIECES)) for j in range(POST_PIECES)]

    def mix(rows):
        lo = _rms_rows(ly_ref[0, rows, :] * _gelu_tanh(lgate_ref[0, rows, :]), lg_ref[...]).astype(BF16)
        return (x_ref[0, rows, :]
                + jnp.dot(ao_ref[0, rows, :], woa_ref[...], preferred_element_type=F32)
                + jnp.dot(lo, wol_ref[...], preferred_element_type=F32))

    def ffn(rows, h):
        xn = _rms_rows(h, fg_ref[...]).astype(BF16)
        fw = wg_ref.shape[1] // n_ffn_chunks
        for c in range(n_ffn_chunks):
            gate = jnp.dot(xn, wg_ref[:, c * fw:(c + 1) * fw], preferred_element_type=F32)
            up = jnp.dot(xn, wu_ref[:, c * fw:(c + 1) * fw], preferred_element_type=F32)
            act = (gate * jax.nn.sigmoid(gate) * up).astype(BF16)
            h = h + jnp.dot(act, wd_ref[c * fw:(c + 1) * fw, :], preferred_element_type=F32)
        o_ref[0, rows, :] = _rms_rows(h, og_ref[...])

    hs = [mix(rows) for rows in pieces]
    for rows, h in zip(pieces, hs):
        ffn(rows, h)


def _post(x, attn_o, lru_y, lru_gate, lru_g, wo_a, wo_l, ffn_g, w_gate, w_up, w_down, out_g, tm):
    B, S, D = x.shape
    F = w_gate.shape[1]
    n_ffn_chunks = 1
    const = lambda b, i: (0, 0)
    resident = functools.partial(pl.BlockSpec, index_map=const, pipeline_mode=pl.Buffered(1))
    row = lambda w: pl.BlockSpec((1, tm, w), lambda b, i: (b, i, 0))
    return pl.pallas_call(
        functools.partial(_post_kernel, n_ffn_chunks=n_ffn_chunks),
        grid=(B, S // tm),
        in_specs=[
            row(D), row(ATTN_WIDTH), row(lru_y.shape[2]), row(lru_y.shape[2]),
            resident((1, lru_y.shape[2])),
            resident(wo_a.shape), resident(wo_l.shape),
            resident((1, D)),
            resident(w_gate.shape), resident(w_up.shape), resident(w_down.shape),
            resident((1, D)),
        ],
        out_specs=row(D),
        out_shape=jax.ShapeDtypeStruct((B, S, D), F32),
        compiler_params=pltpu.CompilerParams(
            dimension_semantics=("parallel", "parallel"), vmem_limit_bytes=VMEM_LIMIT_BYTES),
        name="post",
    )(x, attn_o, lru_y, lru_gate, lru_g, wo_a, wo_l, ffn_g, w_gate, w_up, w_down, out_g)


def _rope_tables(n_tokens):
    n_rows = n_tokens // GRID_W
    row = jnp.repeat(jnp.arange(n_rows), GRID_W).astype(F32)
    col = jnp.tile(jnp.arange(GRID_W), n_rows).astype(F32)
    freqs = ROPE_THETA ** (-jnp.arange(0, ROPE_AXIS_DIM, 2, dtype=F32) / ROPE_AXIS_DIM)
    ar = freqs[:, None] * row[None, :]
    ac = freqs[:, None] * col[None, :]
    cos = jnp.concatenate([jnp.cos(ar), jnp.cos(ar), jnp.cos(ac), jnp.cos(ac)], axis=0)
    sin = jnp.concatenate([-jnp.sin(ar), jnp.sin(ar), -jnp.sin(ac), jnp.sin(ac)], axis=0)
    return cos, sin


def _block_diag_pairs(w):
    nb, bw, _ = w.shape
    w = w.reshape(nb // 2, 2, bw, bw)
    z = jnp.zeros((nb // 2, bw, bw), w.dtype)
    top = jnp.concatenate([w[:, 0], z], axis=2)
    bot = jnp.concatenate([z, w[:, 1]], axis=2)
    return jnp.concatenate([top, bot], axis=1)


def _row_tile(S):
    return 512 if S % 512 == 0 else S


def _trunk(x, prep):
    B, S, D = x.shape
    tm = _row_tile(S)
    cosT, sinT = _rope_tables(S)
    qT, k, vT, lin, lgate = _inproj(x, prep["mix_g"], prep["wqkvT"], prep["wlru"], prep["qg"], prep["kg"],
                                    cosT, sinT, 2 * tm if S % (2 * tm) == 0 else tm)
    lru_y = _lru(prep["lin_meta"], lin, prep["conv_w"], prep["conv_b"], prep["w_gates"],
                 prep["b_gates"], prep["lam"])
    attn_o = _attention(qT, k, vT, prep["k_meta"], prep["vT_meta"], prep["attn_g"], tm)
    return _post(x, attn_o, lru_y, lgate, prep["lru_g"], prep["wo_a"], prep["wo_l"], prep["ffn_g"],
                 prep["w_gate"], prep["w_up"], prep["w_down"], prep["final_g"], tm)


def kernel(x_prompt, x_sample, meta_tokens, norm_mix_g, w_in, q_norm_g, k_norm_g, conv_w, conv_b, lru_w_a, lru_b_a, lru_w_x, lru_b_x, lru_lam, attn_out_g, lru_out_g, w_out, norm_ffn_g, w_gate_up, w_down, final_norm_g):
    assert w_in.shape[0] == 1, "single-layer trunk"
    D = x_prompt.shape[-1]
    lw = D - ATTN_WIDTH
    qkv_w = ATTN_WIDTH + 2 * KV_WIDTH
    w = w_in[0]
    prep = {
        "mix_g": norm_mix_g[0][None],
        "wqkvT": w[:, :qkv_w].T.astype(BF16),
        "wlru": w[:, qkv_w:].astype(BF16),
        "qg": q_norm_g[0][:, None],
        "kg": k_norm_g[0][:, None],
        "conv_w": conv_w[0],
        "conv_b": conv_b[0][None],
        "lam": lru_lam[0],
        "attn_g": attn_out_g[0][:, None],
        "lru_g": lru_out_g[0][None],
        "wo_a": w_out[0][:ATTN_WIDTH].astype(BF16),
        "wo_l": w_out[0][ATTN_WIDTH:].astype(BF16),
        "ffn_g": norm_ffn_g[0][None],
        "w_gate": w_gate_up[0][:, :w_gate_up.shape[2] // 2].astype(BF16),
        "w_up": w_gate_up[0][:, w_gate_up.shape[2] // 2:].astype(BF16),
        "w_down": w_down[0].astype(BF16),
        "final_g": final_norm_g[None],
    }
    wa = [_block_diag_pairs(lru_w_a[0, d]) for d in range(2)]
    wx = [_block_diag_pairs(lru_w_x[0, d]) for d in range(2)]
    prep["w_gates"] = (0.5 * jnp.concatenate([wa[0], wx[0], wa[1], wx[1]], axis=2)).astype(BF16)
    ncg = lw // LRU_LANES
    bias = [b.reshape(ncg, 1, LRU_LANES) for b in (lru_b_a[0, 0], lru_b_x[0, 0], lru_b_a[0, 1], lru_b_x[0, 1])]
    prep["b_gates"] = 0.5 * jnp.concatenate(bias, axis=2)

    meta = jnp.zeros((1, META_PAD, D), F32).at[0, :N_META].set(meta_tokens)
    ones = jnp.ones((HEAD_DIM, META_PAD), F32)
    _, k_m, vT_m, lin_m, _ = _inproj(meta, prep["mix_g"], prep["wqkvT"], prep["wlru"], prep["qg"], prep["kg"],
                                     ones, jnp.zeros_like(ones), META_PAD)
    prep["k_meta"] = k_m[0, :N_META]
    prep["vT_meta"] = vT_m[0, 0, :, :N_META]
    prep["lin_meta"] = lin_m[0, :N_META]
    return _trunk(x_prompt, prep), _trunk(x_sample, prep)
```

```python
import functools

import jax
import jax.numpy as jnp
import numpy as np
from jax import lax
from jax.experimental import pallas as pl
from jax.experimental.pallas import tpu as pltpu

F32 = jnp.float32
BF16 = jnp.bfloat16

N_META = 16
GRID_W = 64
HEAD_DIM = 64
N_Q_HEADS = 8
N_KV_HEADS = 2
Q_PER_KV = N_Q_HEADS // N_KV_HEADS
ATTN_WIDTH = N_Q_HEADS * HEAD_DIM
KV_WIDTH = N_KV_HEADS * HEAD_DIM
LRU_BLOCKS = 8
CONV_W = 4
CONV_LEFT = 2
LRU_C = 8.0
ROPE_AXIS_DIM = HEAD_DIM // 2
ROPE_THETA = 10000.0
EPS = 1e-6
LOG2_E = float(np.log2(np.e))

LANES = 128
SUBLANES = 8
BF16_ROWS = 16
VMEM_LIMIT_BYTES = 56 * 1024 * 1024

LRU_LANES = LANES
SCAN_SEGMENTS = SUBLANES
SCAN_GROUPS = 2
SCAN_UNROLL_MAX = 3
CONV_HALO = SUBLANES
KV_CHUNK = 256
CHUNKS_PER_STEP = 4
QK_AHEAD = 2
SAFE_EXCESS = 64.0
ACC_ROWS = HEAD_DIM + SUBLANES
POST_PIECES = 2
META_PAD = KV_CHUNK


def _rms_rows(x, g):
    ms = jnp.mean(x * x, axis=-1, keepdims=True)
    return x * lax.rsqrt(ms + EPS) * g


def _head_norm_rope(xT, g_col, cos, sin_signed):
    nh = xT.shape[0] // HEAD_DIM
    x = xT.reshape(nh, HEAD_DIM, xT.shape[1])
    ms = jnp.mean(x * x, axis=1, keepdims=True)
    x = x * lax.rsqrt(ms + EPS) * g_col[None]
    q = ROPE_AXIS_DIM // 2
    partner = jnp.concatenate(
        [x[:, q:2 * q], x[:, 0:q], x[:, 3 * q:4 * q], x[:, 2 * q:3 * q]], axis=1)
    x = x * cos[None] + partner * sin_signed[None]
    return x.reshape(xT.shape)


def _inproj_kernel(x_ref, g_ref, wqkv_ref, wlru_ref, qg_ref, kg_ref, cos_ref, sin_ref,
                   qT_ref, k_ref, vT_ref, lin_ref, lgate_ref):
    w = lin_ref.shape[2]

    def project(j):
        rows = slice(j * KV_CHUNK, (j + 1) * KV_CHUNK)
        xn = _rms_rows(x_ref[0, rows, :], g_ref[...]).astype(BF16)
        qkvT = lax.dot_general(wqkv_ref[...], xn, (((1,), (1,)), ((), ())),
                               preferred_element_type=F32)
        lru = jnp.dot(xn, wlru_ref[...], preferred_element_type=F32)
        lin_ref[0, rows, :] = lru[:, :w]
        lgate_ref[0, rows, :] = lru[:, w:]
        return qkvT

    def finish(j, qkvT):
        rows = slice(j * KV_CHUNK, (j + 1) * KV_CHUNK)
        cos = cos_ref[:, rows]
        sin = sin_ref[:, rows]
        qT = _head_norm_rope(qkvT[:ATTN_WIDTH], qg_ref[...], cos, sin) * (HEAD_DIM ** -0.5 * LOG2_E)
        qT_ref[0, :, rows] = qT.astype(BF16)
        kT = _head_norm_rope(qkvT[ATTN_WIDTH:ATTN_WIDTH + KV_WIDTH], kg_ref[...], cos, sin)
        k_ref[0, rows, :] = kT.T.astype(BF16)
        vT_ref[0, j] = qkvT[ATTN_WIDTH + KV_WIDTH:].astype(BF16)

    pieces = vT_ref.shape[1]
    pending = project(0)
    for j in range(pieces):
        current = pending
        if j + 1 < pieces:
            pending = project(j + 1)
        finish(j, current)


def _inproj(x, g, wqkvT, wlru, qg_col, kg_col, cosT, sinT, tm):
    B, S, D = x.shape
    lw = wlru.shape[1] // 2
    nt = S // tm
    const = lambda b, i: (0, 0)
    return pl.pallas_call(
        _inproj_kernel,
        grid=(B, nt),
        in_specs=[
            pl.BlockSpec((1, tm, D), lambda b, i: (b, i, 0)),
            pl.BlockSpec((1, D), const),
            pl.BlockSpec(wqkvT.shape, const),
            pl.BlockSpec(wlru.shape, const),
            pl.BlockSpec((HEAD_DIM, 1), const),
            pl.BlockSpec((HEAD_DIM, 1), const),
            pl.BlockSpec((HEAD_DIM, tm), lambda b, i: (0, i)),
            pl.BlockSpec((HEAD_DIM, tm), lambda b, i: (0, i)),
        ],
        out_specs=[
            pl.BlockSpec((1, ATTN_WIDTH, tm), lambda b, i: (b, 0, i)),
            pl.BlockSpec((1, tm, KV_WIDTH), lambda b, i: (b, i, 0)),
            pl.BlockSpec((1, tm // KV_CHUNK, KV_WIDTH, KV_CHUNK), lambda b, i: (b, i, 0, 0)),
            pl.BlockSpec((1, tm, lw), lambda b, i: (b, i, 0)),
            pl.BlockSpec((1, tm, lw), lambda b, i: (b, i, 0)),
        ],
        out_shape=[
            jax.ShapeDtypeStruct((B, ATTN_WIDTH, S), BF16),
            jax.ShapeDtypeStruct((B, S, KV_WIDTH), BF16),
            jax.ShapeDtypeStruct((B, S // KV_CHUNK, KV_WIDTH, KV_CHUNK), BF16),
            jax.ShapeDtypeStruct((B, S, lw), F32),
            jax.ShapeDtypeStruct((B, S, lw), F32),
        ],
        compiler_params=pltpu.CompilerParams(
            dimension_semantics=("parallel", "parallel"), vmem_limit_bytes=VMEM_LIMIT_BYTES),
        name="inproj",
    )(x, g, wqkvT, wlru, qg_col, kg_col, cosT, sinT)


def _gelu_tanh(x):
    return 0.5 * x * (1.0 + jnp.tanh(np.sqrt(2.0 / np.pi).astype(np.float32) * (x + 0.044715 * (x * x * x))))


def _lru_kernel(xm_ref, x_ref, cw_ref, cb_ref, wg_ref, bg_ref, lam_ref, y_ref,
                X, Af, Uf, Ab, Ub, *, L, T, TY, K):
    S = L - N_META
    H = CONV_HALO
    seg = L // (SCAN_SEGMENTS * SCAN_GROUPS)
    zeros = jnp.zeros((H, LRU_LANES), F32)
    X[0:H] = zeros
    X[H + L:H + L + H] = zeros
    X[H:H + N_META] = xm_ref[...]
    X[H + N_META:H + L] = x_ref[0]

    lam = lam_ref[...]
    z = -lam
    softplus = jnp.maximum(z, 0.0) + jnp.log1p(jnp.exp(-jnp.abs(z)))
    half_decay = (0.5 * LRU_C) * softplus
    cw = cw_ref[...]
    cb = cb_ref[...]
    bg = bg_ref[...]

    def gate_chunk(i, carry):
        r0 = pl.multiple_of(i * T, SUBLANES)
        c = cb
        for j in range(CONV_W):
            c = c + X[pl.ds(r0 + (H - CONV_LEFT + j), T), :] * cw[j:j + 1]
        gh = jnp.dot(c.astype(BF16), wg_ref[...], preferred_element_type=F32) + bg
        for d, (A, U) in enumerate(((Af, Uf), (Ab, Ub))):
            base = 2 * d * LRU_LANES
            t_r = jnp.tanh(gh[:, base:base + LRU_LANES])
            t_i = jnp.tanh(gh[:, base + LRU_LANES:base + 2 * LRU_LANES])
            hd = half_decay[d:d + 1]
            nla = hd * t_r + hd
            a = jnp.exp2(nla * (-LOG2_E))
            x = jnp.tanh(nla) * (1.0 + a * a)
            mult = jnp.where(x == 0.0, 0.0, x * lax.rsqrt(x))
            A[pl.ds(r0, T), :] = a
            U[pl.ds(r0, T), :] = (mult * c) * (0.5 * t_i + 0.5)
        return carry

    lax.fori_loop(0, L // T, gate_chunk, 0, unroll=True)

    G = SCAN_GROUPS

    def seg_rows(g, j):
        return pl.ds(g * SCAN_SEGMENTS * seg + j, SCAN_SEGMENTS, stride=seg)

    def scan_block(jb, carry):
        state = [list(c) for c in carry]
        j0 = jb * K
        fwd = [[(Af[seg_rows(g, j0 + k), :], Uf[seg_rows(g, j0 + k), :]) for k in range(K)] for g in range(G)]
        bwd = [[(Ab[seg_rows(g, seg - 1 - j0 - k), :], Ub[seg_rows(g, seg - 1 - j0 - k), :])
                for k in range(K)] for g in range(G)]
        out = []
        for k in range(K):
            for g in range(G):
                hf, pf, hb, pb = state[g]
                a, u = fwd[g][k]
                hf = a * hf + u
                pf = a * pf
                a, u = bwd[g][k]
                hb = a * hb + u
                pb = a * pb
                state[g] = [hf, pf, hb, pb]
                out.append((g, k, hf, pf, hb, pb))
        for g, k, hf, pf, hb, pb in out:
            Uf[seg_rows(g, j0 + k), :] = hf
            Af[seg_rows(g, j0 + k), :] = pf
            Ub[seg_rows(g, seg - 1 - j0 - k), :] = hb
            Ab[seg_rows(g, seg - 1 - j0 - k), :] = pb
        return tuple(tuple(c) for c in state)

    zero = jnp.zeros((SCAN_SEGMENTS, LRU_LANES), F32)
    one = jnp.ones((SCAN_SEGMENTS, LRU_LANES), F32)
    ends = lax.fori_loop(0, seg // K, scan_block, tuple((zero, one, zero, one) for _ in range(G)))

    order = [(g, r) for g in range(G) for r in range(SCAN_SEGMENTS)]
    row = jnp.zeros((1, LRU_LANES), F32)
    rows = {}
    for g, r in order:
        rows[g, r] = row
        row = ends[g][0][r:r + 1] + ends[g][1][r:r + 1] * row
    cin_f = [jnp.concatenate([rows[g, r] for r in range(SCAN_SEGMENTS)], axis=0) for g in range(G)]
    row = jnp.zeros((1, LRU_LANES), F32)
    rows = {}
    for g, r in reversed(order):
        rows[g, r] = row
        row = ends[g][2][r:r + 1] + ends[g][3][r:r + 1] * row
    cin_b = [jnp.concatenate([rows[g, r] for r in range(SCAN_SEGMENTS)], axis=0) for g in range(G)]

    def fix_block(jb, carry):
        j0 = jb * K
        h = [(g, k, Uf[seg_rows(g, j0 + k), :] + Af[seg_rows(g, j0 + k), :] * cin_f[g]
              + Ub[seg_rows(g, j0 + k), :] + Ab[seg_rows(g, j0 + k), :] * cin_b[g])
             for k in range(K) for g in range(G)]
        for g, k, v in h:
            Uf[seg_rows(g, j0 + k), :] = v
        return carry

    lax.fori_loop(0, seg // K, fix_block, 0)

    def out_chunk(i, carry):
        r0 = pl.multiple_of(i * TY, SUBLANES)
        y_ref[0, pl.ds(r0, TY), :] = Uf[pl.ds(N_META + r0, TY), :]
        return carry

    lax.fori_loop(0, S // TY, out_chunk, 0)


def _pick_chunk(n, align, cap):
    best = align
    for t in range(align, cap + 1, align):
        if n % t == 0:
            best = t
    return best


def _lru(xin_meta, xin, cw, cb, wg, bg, lam):
    B, S, W = xin.shape
    L = S + N_META
    assert L % (SCAN_SEGMENTS * SCAN_GROUPS) == 0 and W % LRU_LANES == 0
    ncg = W // LRU_LANES
    T = _pick_chunk(L, BF16_ROWS, 512)
    TY = _pick_chunk(S, SUBLANES, 512)
    assert L % T == 0 and S % TY == 0
    scratch_rows = L + 2 * CONV_HALO
    K = _pick_chunk(L // (SCAN_SEGMENTS * SCAN_GROUPS), 1, SCAN_UNROLL_MAX)
    kern = functools.partial(_lru_kernel, L=L, T=T, TY=TY, K=K)
    return pl.pallas_call(
        kern,
        grid=(B, ncg),
        in_specs=[
            pl.BlockSpec((N_META, LRU_LANES), lambda b, c: (0, c)),
            pl.BlockSpec((1, S, LRU_LANES), lambda b, c: (b, 0, c)),
            pl.BlockSpec((CONV_W, LRU_LANES), lambda b, c: (0, c)),
            pl.BlockSpec((1, LRU_LANES), lambda b, c: (0, c)),
            pl.BlockSpec((None, LRU_LANES, 4 * LRU_LANES), lambda b, c: (c, 0, 0)),
            pl.BlockSpec((None, 1, 4 * LRU_LANES), lambda b, c: (c, 0, 0)),
            pl.BlockSpec((2, LRU_LANES), lambda b, c: (0, c)),
        ],
        out_specs=pl.BlockSpec((1, S, LRU_LANES), lambda b, c: (b, 0, c)),
        out_shape=jax.ShapeDtypeStruct((B, S, W), F32),
        scratch_shapes=[pltpu.VMEM((scratch_rows, LRU_LANES), F32)]
        + [pltpu.VMEM((L, LRU_LANES), F32) for _ in range(4)],
        compiler_params=pltpu.CompilerParams(
            dimension_semantics=("parallel", "parallel"), vmem_limit_bytes=VMEM_LIMIT_BYTES),
        name="lru",
    )(xin_meta, xin, cw, cb, wg, bg, lam)


def _attn_kernel(qT_ref, k_ref, vT_ref, km_ref, vTm_ref, og_ref, o_ref,
                 qext, m_s, acc_s, s_carry, smax_carry):
    tq = qT_ref.shape[2]
    nchunks = vT_ref.shape[1]
    zeros = jnp.zeros((HEAD_DIM, tq), BF16)
    for h in range(N_Q_HEADS):
        qh = qT_ref[0, h * HEAD_DIM:(h + 1) * HEAD_DIM, :]
        parts = [zeros] * N_KV_HEADS
        parts[h // Q_PER_KV] = qh
        qext[h] = jnp.concatenate(parts, axis=0)

    def with_ones(vT_all, g):
        ones = jnp.ones((BF16_ROWS, vT_all.shape[1]), BF16)
        return jnp.concatenate([vT_all[g * HEAD_DIM:(g + 1) * HEAD_DIM], ones], axis=0)

    def scores(c, h):
        kc = k_ref[0, pl.ds(pl.multiple_of(c * KV_CHUNK, KV_CHUNK), KV_CHUNK), :]
        return jnp.dot(kc, qext[h], preferred_element_type=F32)

    def init_from_meta():
        km = km_ref[...]
        vTm = [with_ones(vTm_ref[...], g) for g in range(N_KV_HEADS)]
        s_meta = [jnp.dot(km, qext[h], preferred_element_type=F32) for h in range(N_Q_HEADS)]
        p_meta = []
        for h in range(N_Q_HEADS):
            m = jnp.max(s_meta[h], axis=0, keepdims=True)
            m_s[h] = m
            p_meta.append(jnp.exp2(s_meta[h] - m).astype(BF16))
        for h in range(N_Q_HEADS):
            acc_s[h] = jnp.dot(vTm[h // Q_PER_KV], p_meta[h], preferred_element_type=F32)[:ACC_ROWS]

    for n in range(QK_AHEAD):
        first = scores(n // N_Q_HEADS, n % N_Q_HEADS)
        s_carry[n] = first
        smax_carry[n] = jnp.max(first, axis=0, keepdims=True)
    init_from_meta()

    def values(c):
        return [with_ones(vT_ref[0, c], g) for g in range(N_KV_HEADS)]

    def exact_chunk(c, carry):
        vc = values(c)
        for h in range(N_Q_HEADS):
            s = scores(c, h)
            m_old = m_s[h]
            m_new = jnp.maximum(m_old, jnp.max(s, axis=0, keepdims=True))
            p = jnp.exp2(s - m_new)
            pv = jnp.dot(vc[h // Q_PER_KV], p.astype(BF16), preferred_element_type=F32)
            acc_s[h] = jnp.exp2(m_old - m_new) * acc_s[h] + pv[:ACC_ROWS]
            m_s[h] = m_new
        return carry

    def fast_trip(i, excess):
        vcs = [values(i * CHUNKS_PER_STEP + j) for j in range(CHUNKS_PER_STEP)]
        units = [(j, h) for j in range(CHUNKS_PER_STEP) for h in range(N_Q_HEADS)]
        m_start = [m_s[h] for h in range(N_Q_HEADS)]
        pending = [(s_carry[n], smax_carry[n]) for n in range(QK_AHEAD)]
        for n in range(QK_AHEAD):
            excess = jnp.maximum(excess, pending[n][1] - m_start[units[n][1]])
        for n, (j, h) in enumerate(units):
            s, s_max = pending.pop(0)
            if n + QK_AHEAD < len(units):
                ja, ha = units[n + QK_AHEAD]
                ahead = scores(i * CHUNKS_PER_STEP + ja, ha)
                ahead_max = jnp.max(ahead, axis=0, keepdims=True)
                excess = jnp.maximum(excess, ahead_max - m_start[ha])
                pending.append((ahead, ahead_max))
            else:
                nxt = n + QK_AHEAD - len(units)
                c_next = jnp.minimum((i + 1) * CHUNKS_PER_STEP + units[nxt][0], nchunks - 1)
                ahead = scores(c_next, units[nxt][1])
                s_carry[nxt] = ahead
                smax_carry[nxt] = jnp.max(ahead, axis=0, keepdims=True)
            m_ref = m_s[h]
            p = jnp.exp2(s - m_ref)
            pv = jnp.dot(vcs[j][h // Q_PER_KV], p.astype(BF16), preferred_element_type=F32)
            m_new = jnp.maximum(m_ref, s_max)
            acc_s[h] = (acc_s[h] + pv[:ACC_ROWS]) * jnp.exp2(m_ref - m_new)
            m_s[h] = m_new
        return excess

    assert nchunks % CHUNKS_PER_STEP == 0
    excess = lax.fori_loop(0, nchunks // CHUNKS_PER_STEP, fast_trip, jnp.full((1, tq), -jnp.inf, F32),
                           unroll=True)

    @pl.when(jnp.logical_not(jnp.max(excess) <= SAFE_EXCESS))
    def _():
        init_from_meta()
        lax.fori_loop(0, nchunks, exact_chunk, 0)

    oT = jnp.concatenate([acc_s[h, :HEAD_DIM] / acc_s[h, HEAD_DIM:HEAD_DIM + 1]
                          for h in range(N_Q_HEADS)], axis=0)
    ms = jnp.mean(oT * oT, axis=0, keepdims=True)
    oT = oT * lax.rsqrt(ms + EPS) * og_ref[...]
    o_ref[0] = oT.T.astype(BF16)


def _attention(qT, k, vT, k_meta, vT_meta, og_col, tq):
    B, _, S = qT.shape
    nchunks = S // KV_CHUNK
    return pl.pallas_call(
        _attn_kernel,
        grid=(B, S // tq),
        in_specs=[
            pl.BlockSpec((1, ATTN_WIDTH, tq), lambda b, i: (b, 0, i)),
            pl.BlockSpec((1, S, KV_WIDTH), lambda b, i: (b, 0, 0)),
            pl.BlockSpec((1, nchunks, KV_WIDTH, KV_CHUNK), lambda b, i: (b, 0, 0, 0)),
            pl.BlockSpec((N_META, KV_WIDTH), lambda b, i: (0, 0)),
            pl.BlockSpec((KV_WIDTH, N_META), lambda b, i: (0, 0)),
            pl.BlockSpec((ATTN_WIDTH, 1), lambda b, i: (0, 0)),
        ],
        out_specs=pl.BlockSpec((1, tq, ATTN_WIDTH), lambda b, i: (b, i, 0)),
        out_shape=jax.ShapeDtypeStruct((B, S, ATTN_WIDTH), BF16),
        scratch_shapes=[
            pltpu.VMEM((N_Q_HEADS, KV_WIDTH, tq), BF16),
            pltpu.VMEM((N_Q_HEADS, 1, tq), F32),
            pltpu.VMEM((N_Q_HEADS, ACC_ROWS, tq), F32),
            pltpu.VMEM((QK_AHEAD, KV_CHUNK, tq), F32),
            pltpu.VMEM((QK_AHEAD, 1, tq), F32),
        ],
        compiler_params=pltpu.CompilerParams(
            dimension_semantics=("parallel", "parallel"), vmem_limit_bytes=VMEM_LIMIT_BYTES),
        name="attention",
    )(qT, k, vT, k_meta, vT_meta, og_col)


def _post_kernel(x_ref, ao_ref, ly_ref, lgate_ref, lg_ref, woa_ref, wol_ref, fg_ref, wg_ref, wu_ref, wd_ref,
                 og_ref, o_ref, *, n_ffn_chunks):
    tm = x_ref.shape[1]
    pieces = [slice(j * (tm // POST_PIECES), (j + 1) * (tm // POST_PIECES)) for j in range(POST_PIECES)]

    def mix(rows):
        lo = _rms_rows(ly_ref[0, rows, :] * _gelu_tanh(lgate_ref[0, rows, :]), lg_ref[...]).astype(BF16)
        return (x_ref[0, rows, :]
                + jnp.dot(ao_ref[0, rows, :], woa_ref[...], preferred_element_type=F32)
                + jnp.dot(lo, wol_ref[...], preferred_element_type=F32))

    def ffn(rows, h):
        xn = _rms_rows(h, fg_ref[...]).astype(BF16)
        fw = wg_ref.shape[1] // n_ffn_chunks
        for c in range(n_ffn_chunks):
            gate = jnp.dot(xn, wg_ref[:, c * fw:(c + 1) * fw], preferred_element_type=F32)
            up = jnp.dot(xn, wu_ref[:, c * fw:(c + 1) * fw], preferred_element_type=F32)
            act = (gate * jax.nn.sigmoid(gate) * up).astype(BF16)
            h = h + jnp.dot(act, wd_ref[c * fw:(c + 1) * fw, :], preferred_element_type=F32)
        o_ref[0, rows, :] = _rms_rows(h, og_ref[...])

    hs = [mix(rows) for rows in pieces]
    for rows, h in zip(pieces, hs):
        ffn(rows, h)


def _post(x, attn_o, lru_y, lru_gate, lru_g, wo_a, wo_l, ffn_g, w_gate, w_up, w_down, out_g, tm):
    B, S, D = x.shape
    F = w_gate.shape[1]
    n_ffn_chunks = 1
    const = lambda b, i: (0, 0)
    resident = functools.partial(pl.BlockSpec, index_map=const, pipeline_mode=pl.Buffered(1))
    row = lambda w: pl.BlockSpec((1, tm, w), lambda b, i: (b, i, 0))
    return pl.pallas_call(
        functools.partial(_post_kernel, n_ffn_chunks=n_ffn_chunks),
        grid=(B, S // tm),
        in_specs=[
            row(D), row(ATTN_WIDTH), row(lru_y.shape[2]), row(lru_y.shape[2]),
            resident((1, lru_y.shape[2])),
            resident(wo_a.shape), resident(wo_l.shape),
            resident((1, D)),
            resident(w_gate.shape), resident(w_up.shape), resident(w_down.shape),
            resident((1, D)),
        ],
        out_specs=row(D),
        out_shape=jax.ShapeDtypeStruct((B, S, D), F32),
        compiler_params=pltpu.CompilerParams(
            dimension_semantics=("parallel", "parallel"), vmem_limit_bytes=VMEM_LIMIT_BYTES),
        name="post",
    )(x, attn_o, lru_y, lru_gate, lru_g, wo_a, wo_l, ffn_g, w_gate, w_up, w_down, out_g)


def _rope_tables(n_tokens):
    n_rows = n_tokens // GRID_W
    row = jnp.repeat(jnp.arange(n_rows), GRID_W).astype(F32)
    col = jnp.tile(jnp.arange(GRID_W), n_rows).astype(F32)
    freqs = ROPE_THETA ** (-jnp.arange(0, ROPE_AXIS_DIM, 2, dtype=F32) / ROPE_AXIS_DIM)
    ar = freqs[:, None] * row[None, :]
    ac = freqs[:, None] * col[None, :]
    cos = jnp.concatenate([jnp.cos(ar), jnp.cos(ar), jnp.cos(ac), jnp.cos(ac)], axis=0)
    sin = jnp.concatenate([-jnp.sin(ar), jnp.sin(ar), -jnp.sin(ac), jnp.sin(ac)], axis=0)
    return cos, sin


def _block_diag_pairs(w):
    nb, bw, _ = w.shape
    w = w.reshape(nb // 2, 2, bw, bw)
    z = jnp.zeros((nb // 2, bw, bw), w.dtype)
    top = jnp.concatenate([w[:, 0], z], axis=2)
    bot = jnp.concatenate([z, w[:, 1]], axis=2)
    return jnp.concatenate([top, bot], axis=1)


def _row_tile(S):
    return 512 if S % 512 == 0 else S


def _trunk(x, prep):
    B, S, D = x.shape
    tm = _row_tile(S)
    cosT, sinT = _rope_tables(S)
    qT, k, vT, lin, lgate = _inproj(x, prep["mix_g"], prep["wqkvT"], prep["wlru"], prep["qg"], prep["kg"],
                                    cosT, sinT, 2 * tm if S % (2 * tm) == 0 else tm)
    lru_y = _lru(prep["lin_meta"], lin, prep["conv_w"], prep["conv_b"], prep["w_gates"],
                 prep["b_gates"], prep["lam"])
    attn_o = _attention(qT, k, vT, prep["k_meta"], prep["vT_meta"], prep["attn_g"], tm)
    return _post(x, attn_o, lru_y, lgate, prep["lru_g"], prep["wo_a"], prep["wo_l"], prep["ffn_g"],
                 prep["w_gate"], prep["w_up"], prep["w_down"], prep["final_g"], tm)


def kernel(x_prompt, x_sample, meta_tokens, norm_mix_g, w_in, q_norm_g, k_norm_g, conv_w, conv_b, lru_w_a, lru_b_a, lru_w_x, lru_b_x, lru_lam, attn_out_g, lru_out_g, w_out, norm_ffn_g, w_gate_up, w_down, final_norm_g):
    assert w_in.shape[0] == 1, "single-layer trunk"
    D = x_prompt.shape[-1]
    lw = D - ATTN_WIDTH
    qkv_w = ATTN_WIDTH + 2 * KV_WIDTH
    w = w_in[0]
    prep = {
        "mix_g": norm_mix_g[0][None],
        "wqkvT": w[:, :qkv_w].T.astype(BF16),
        "wlru": w[:, qkv_w:].astype(BF16),
        "qg": q_norm_g[0][:, None],
        "kg": k_norm_g[0][:, None],
        "conv_w": conv_w[0],
        "conv_b": conv_b[0][None],
        "lam": lru_lam[0],
        "attn_g": attn_out_g[0][:, None],
        "lru_g": lru_out_g[0][None],
        "wo_a": w_out[0][:ATTN_WIDTH].astype(BF16),
        "wo_l": w_out[0][ATTN_WIDTH:].astype(BF16),
        "ffn_g": norm_ffn_g[0][None],
        "w_gate": w_gate_up[0][:, :w_gate_up.shape[2] // 2].astype(BF16),
        "w_up": w_gate_up[0][:, w_gate_up.shape[2] // 2:].astype(BF16),
        "w_down": w_down[0].astype(BF16),
        "final_g": final_norm_g[None],
    }
    wa = [_block_diag_pairs(lru_w_a[0, d]) for d in range(2)]
    wx = [_block_diag_pairs(lru_w_x[0, d]) for d in range(2)]
    prep["w_gates"] = (0.5 * jnp.concatenate([wa[0], wx[0], wa[1], wx[1]], axis=2)).astype(BF16)
    ncg = lw // LRU_LANES
    bias = [b.reshape(ncg, 1, LRU_LANES) for b in (lru_b_a[0, 0], lru_b_x[0, 0], lru_b_a[0, 1], lru_b_x[0, 1])]
    prep["b_gates"] = 0.5 * jnp.concatenate(bias, axis=2)

    meta = jnp.zeros((1, META_PAD, D), F32).at[0, :N_META].set(meta_tokens)
    ones = jnp.ones((HEAD_DIM, META_PAD), F32)
    _, k_m, vT_m, lin_m, _ = _inproj(meta, prep["mix_g"], prep["wqkvT"], prep["wlru"], prep["qg"], prep["kg"],
                                     ones, jnp.zeros_like(ones), META_PAD)
    prep["k_meta"] = k_m[0, :N_META]
    prep["vT_meta"] = vT_m[0, 0, :, :N_META]
    prep["lin_meta"] = lin_m[0, :N_META]
    return _trunk(x_prompt, prep), _trunk(x_sample, prep)
```
